```python
import math
import jax, jax.numpy as jnp
from jax import lax
import numpy as np

D_MODEL = 1024
BATCH = 4
SEQ = 8192
DEPTH = 2

CHUNK = 64
D_MIX = D_MODEL
DN_HEADS = 4
DN_HEAD_DIM = 128
DN_WIDTH = DN_HEADS * DN_HEAD_DIM
SHORT_CONV = 4
CONF_WIDTH = D_MIX - DN_WIDTH
CONF_KERNEL = 31
C_IN = 4 * DN_WIDTH + 2 * DN_HEADS + 2 * CONF_WIDTH
N_GROUPS = 4
EXPERTS_PER_GROUP = 8
N_EXPERTS = N_GROUPS * EXPERTS_PER_GROUP
TOP_K = 2
D_EXPERT = 512
MOE_BLOCK = 128
EPS = 1e-6

kernel_name = 'hybrid_deltanet_conformer_hiermoe'


def rmsnorm(x, w):
    xf = x.astype(jnp.float32)
    y = xf * lax.rsqrt(jnp.mean(xf * xf, axis=-1, keepdims=True) + EPS)
    return (y * w.astype(jnp.float32)).astype(x.dtype)


def layernorm(x, g, b):
    xf = x.astype(jnp.float32)
    mu = jnp.mean(xf, axis=-1, keepdims=True)
    var = jnp.mean(jnp.square(xf - mu), axis=-1, keepdims=True)
    y = (xf - mu) * lax.rsqrt(var + EPS) * g.astype(jnp.float32) + b.astype(jnp.float32)
    return y.astype(x.dtype)


def l2norm(x):
    return x * lax.rsqrt(jnp.sum(x * x, axis=-1, keepdims=True) + EPS)


def causal_dwconv(x, w):
    K, C = w.shape
    return lax.conv_general_dilated(
        x, w[:, None, :].astype(x.dtype), window_strides=(1,), padding=[(K - 1, 0)],
        dimension_numbers=('NWC', 'WIO', 'NWC'), feature_group_count=C)


def chunked_gated_delta_rule(q, k, v, g, beta):
    Bn, L, H, dk = q.shape
    dv = v.shape[-1]
    N = L // CHUNK
    q = l2norm(q) * (dk ** -0.5)
    k = l2norm(k)

    def to_chunks(t):
        t = t.reshape((Bn, N, CHUNK, H) + t.shape[3:])
        return jnp.moveaxis(t, 3, 1)

    qc, kc, vc, gc, bc = (to_chunks(t) for t in (q, k, v, g, beta))
    gc = jnp.cumsum(gc, axis=-1)
    incl = jnp.tril(jnp.ones((CHUNK, CHUNK), bool))
    strict = jnp.tril(jnp.ones((CHUNK, CHUNK), bool), -1)
    decay = jnp.exp(jnp.where(incl, gc[..., :, None] - gc[..., None, :], -jnp.inf))
    a_mat = jnp.where(strict, bc[..., :, None] * jnp.einsum('bhncd,bhnsd->bhncs', kc, kc) * decay, 0.0)
    rhs = jnp.concatenate([bc[..., None] * vc, (bc * jnp.exp(gc))[..., None] * kc], axis=-1)
    sol = lax.linalg.triangular_solve(a_mat + jnp.eye(CHUNK, dtype=a_mat.dtype), rhs,
                                      left_side=True, lower=True, unit_diagonal=True)
    u_c, w_c = sol[..., :dv], sol[..., dv:]
    qk = jnp.einsum('bhncd,bhnsd->bhncs', qc, kc) * decay
    q_dec = qc * jnp.exp(gc)[..., None]
    k_dec = kc * jnp.exp(gc[..., -1:] - gc)[..., None]
    g_last = jnp.exp(gc[..., -1])
    xs = tuple(jnp.moveaxis(t, 2, 0) for t in (q_dec, k_dec, u_c, w_c, qk, g_last))

    def step(S, inp):
        q_i, k_i, u_i, w_i, qk_i, gl_i = inp
        v_new = u_i - jnp.einsum('bhcd,bhde->bhce', w_i, S)
        o_i = jnp.einsum('bhcd,bhde->bhce', q_i, S) + jnp.einsum('bhcs,bhse->bhce', qk_i, v_new)
        S = S * gl_i[..., None, None] + jnp.einsum('bhcd,bhce->bhde', k_i, v_new)
        return S, o_i

    S0 = jnp.zeros((Bn, H, dk, dv), jnp.float32)
    _, o = lax.scan(step, S0, xs)
    o = jnp.moveaxis(o, 0, 2)
    return jnp.moveaxis(o, 1, 3).reshape(Bn, L, H, dv)


def gated_deltanet(qkv, z, b_logit, a_logit, conv_w, a_log, dt_bias, o_norm):
    Bn, L, _ = qkv.shape
    out_dtype = qkv.dtype
    qkv = jax.nn.silu(causal_dwconv(qkv, conv_w)).astype(jnp.float32)
    q, k, v = jnp.split(qkv, 3, axis=-1)
    hs = (Bn, L, DN_HEADS, DN_HEAD_DIM)
    q, k, v = q.reshape(hs), k.reshape(hs), v.reshape(hs)
    beta = jax.nn.sigmoid(b_logit.astype(jnp.float32))
    g = -jnp.exp(a_log.astype(jnp.float32)) * jax.nn.softplus(
        a_logit.astype(jnp.float32) + dt_bias.astype(jnp.float32))
    o = chunked_gated_delta_rule(q, k, v, g, beta)
    o = o * lax.rsqrt(jnp.mean(o * o, axis=-1, keepdims=True) + EPS) * o_norm.astype(jnp.float32)
    o = o * jax.nn.silu(z.astype(jnp.float32).reshape(hs))
    return o.reshape(Bn, L, DN_WIDTH).astype(out_dtype)


def conformer_conv(u, w_dw, b_dw, ln_g, ln_b):
    glu = u[..., :CONF_WIDTH] * jax.nn.sigmoid(u[..., CONF_WIDTH:])
    c = causal_dwconv(glu, w_dw) + b_dw.astype(glu.dtype)
    c = layernorm(c, ln_g, ln_b)
    return jax.nn.silu(c)


def hier_moe(h, w_rg, b_rg, w_re, b_re, w_gate, w_up, w_down):
    T, D = h.shape
    hf = h.astype(jnp.float32)
    p_grp = jax.nn.softmax(hf @ w_rg.astype(jnp.float32) + b_rg.astype(jnp.float32), axis=-1)
    p_g, g_sel = lax.top_k(p_grp, 1)
    e_logits = (hf @ w_re.astype(jnp.float32) + b_re.astype(jnp.float32)).reshape(
        T, N_GROUPS, EXPERTS_PER_GROUP)
    e_logits = jnp.take_along_axis(e_logits, g_sel[:, :, None], axis=1)[:, 0]
    p_e = jax.nn.softmax(e_logits, axis=-1)
    top_p, top_i = lax.top_k(p_e, TOP_K)
    gates = p_g * top_p / jnp.sum(top_p, axis=-1, keepdims=True)
    expert_id = g_sel * EXPERTS_PER_GROUP + top_i

    A = T * TOP_K
    flat_e = expert_id.reshape(A)
    flat_tok = jnp.repeat(jnp.arange(T, dtype=jnp.int32), TOP_K)
    flat_w = gates.reshape(A)
    order = jnp.argsort(flat_e)
    e_s, tok_s, w_s = flat_e[order], flat_tok[order], flat_w[order]
    counts = jnp.bincount(flat_e, length=N_EXPERTS)
    padded = (counts + MOE_BLOCK - 1) // MOE_BLOCK * MOE_BLOCK
    starts = jnp.cumsum(counts) - counts
    pad_ends = jnp.cumsum(padded)
    pad_starts = pad_ends - padded
    dest = pad_starts[e_s] + jnp.arange(A, dtype=jnp.int32) - starts[e_s]
    n_blocks = -(-(A + N_EXPERTS * MOE_BLOCK) // MOE_BLOCK)
    P = n_blocks * MOE_BLOCK
    slot_tok = jnp.full((P,), T, jnp.int32).at[dest].set(tok_s)
    block_expert = jnp.minimum(
        jnp.searchsorted(pad_ends, jnp.arange(n_blocks, dtype=jnp.int32) * MOE_BLOCK, side='right'),
        N_EXPERTS - 1)
    h_pad = jnp.concatenate([h, jnp.zeros((1, D), h.dtype)], axis=0)

    def run_block(args):
        toks, e = args
        xb = h_pad[toks]
        return (jax.nn.silu(xb @ w_gate[e]) * (xb @ w_up[e])) @ w_down[e]

    y = lax.map(run_block, (slot_tok.reshape(n_blocks, MOE_BLOCK), block_expert)).reshape(P, D)
    return jnp.zeros_like(h).at[tok_s].add(y[dest] * w_s[:, None].astype(h.dtype))


def setup_inputs(seed: int = 0) -> dict:
    key = jax.random.key(seed)
    ks = jax.random.split(key, 21)
    f32 = jnp.float32

    def nrm(k, shape, scale):
        return jax.random.normal(k, shape, f32) * scale

    def gain(k, shape):
        return 1.0 + 0.01 * jax.random.normal(k, shape, f32)

    x = nrm(ks[0], (BATCH, SEQ, D_MODEL), 1.0)
    norm_mix = gain(ks[1], (DEPTH, D_MODEL))
    w_in = nrm(ks[2], (DEPTH, D_MODEL, C_IN), D_MODEL ** -0.5)
    conv_qkv = nrm(ks[3], (DEPTH, SHORT_CONV, 3 * DN_WIDTH), SHORT_CONV ** -0.5)
    a_log = jnp.log(jax.random.uniform(ks[4], (DEPTH, DN_HEADS), f32, 1.0, 16.0))
    dt = jnp.exp(jax.random.uniform(ks[5], (DEPTH, DN_HEADS), f32, math.log(1e-3), math.log(1e-1)))
    dt_bias = dt + jnp.log(-jnp.expm1(-dt))
    o_norm = gain(ks[6], (DEPTH, DN_HEAD_DIM))
    w_dw = nrm(ks[7], (DEPTH, CONF_KERNEL, CONF_WIDTH), CONF_KERNEL ** -0.5)
    b_dw = nrm(ks[8], (DEPTH, CONF_WIDTH), 0.01)
    ln_g = gain(ks[9], (DEPTH, CONF_WIDTH))
    ln_b = nrm(ks[10], (DEPTH, CONF_WIDTH), 0.01)
    w_out = nrm(ks[11], (DEPTH, D_MIX, D_MODEL), D_MIX ** -0.5)
    norm_ffn = gain(ks[12], (DEPTH, D_MODEL))
    w_rg = nrm(ks[13], (DEPTH, D_MODEL, N_GROUPS), D_MODEL ** -0.5)
    b_rg = nrm(ks[14], (DEPTH, N_GROUPS), 0.01)
    w_re = nrm(ks[15], (DEPTH, D_MODEL, N_EXPERTS), D_MODEL ** -0.5)
    b_re = nrm(ks[16], (DEPTH, N_EXPERTS), 0.01)
    w_gate = nrm(ks[17], (DEPTH, N_EXPERTS, D_MODEL, D_EXPERT), D_MODEL ** -0.5)
    w_up = nrm(ks[18], (DEPTH, N_EXPERTS, D_MODEL, D_EXPERT), D_MODEL ** -0.5)
    w_down = nrm(ks[19], (DEPTH, N_EXPERTS, D_EXPERT, D_MODEL), D_EXPERT ** -0.5)
    norm_final = gain(ks[20], (D_MODEL,))
    return {'x': x, 'norm_mix': norm_mix, 'w_in': w_in, 'conv_qkv': conv_qkv, 'a_log': a_log,
            'dt_bias': dt_bias, 'o_norm': o_norm, 'w_dw': w_dw, 'b_dw': b_dw, 'ln_g': ln_g,
            'ln_b': ln_b, 'w_out': w_out, 'norm_ffn': norm_ffn, 'w_rg': w_rg, 'b_rg': b_rg,
            'w_re': w_re, 'b_re': b_re, 'w_gate': w_gate, 'w_up': w_up, 'w_down': w_down,
            'norm_final': norm_final}


def reference(x, norm_mix, w_in, conv_qkv, a_log, dt_bias, o_norm, w_dw, b_dw, ln_g, ln_b,
              w_out, norm_ffn, w_rg, b_rg, w_re, b_re, w_gate, w_up, w_down, norm_final):
    Bn, L, D = x.shape
    o_z = 3 * DN_WIDTH
    o_b = 4 * DN_WIDTH
    o_a = o_b + DN_HEADS
    o_c = o_a + DN_HEADS
    for l in range(DEPTH):
        h = rmsnorm(x, norm_mix[l])
        u = h @ w_in[l]
        y_dn = gated_deltanet(u[..., :o_z], u[..., o_z:o_b], u[..., o_b:o_a], u[..., o_a:o_c],
                              conv_qkv[l], a_log[l], dt_bias[l], o_norm[l])
        y_cf = conformer_conv(u[..., o_c:], w_dw[l], b_dw[l], ln_g[l], ln_b[l])
        x = x + (jnp.concatenate([y_dn, y_cf], axis=-1) @ w_out[l]).astype(x.dtype)
        h = rmsnorm(x, norm_ffn[l])
        y = hier_moe(h.reshape(Bn * L, D), w_rg[l], b_rg[l], w_re[l], b_re[l],
                     w_gate[l], w_up[l], w_down[l])
        x = x + y.reshape(Bn, L, D).astype(x.dtype)
    return rmsnorm(x, norm_final)
```

```python
import functools

import jax
import jax.numpy as jnp
from jax import lax
from jax.experimental import pallas as pl
from jax.experimental.pallas import tpu as pltpu

EPS = 1e-6
CHUNK = 64
DN_HEADS = 4
DN_HEAD_DIM = 128
DN_WIDTH = DN_HEADS * DN_HEAD_DIM
SHORT_CONV = 4
CONF_WIDTH = 512
CONF_KERNEL = 31
N_GROUPS = 4
EXPERTS_PER_GROUP = 8
N_EXPERTS = N_GROUPS * EXPERTS_PER_GROUP
D_EXPERT = 512
LANES = 128
SUBLANES = 8
GROUP_LANE0 = N_EXPERTS
BETA_LANE0 = 0
DECAY_LANE0 = DN_HEADS
SLOT_BLOCK = 256
VMEM_LIMIT = 56 * 1024 * 1024

F32 = jnp.float32
BF16 = jnp.bfloat16
HIGHEST = lax.Precision.HIGHEST
NT_DIMS = (((1,), (1,)), ((), ()))
TN_DIMS = (((0,), (0,)), ((), ()))


def _params(*semantics):
    return pltpu.CompilerParams(dimension_semantics=semantics, vmem_limit_bytes=VMEM_LIMIT)


def _sigmoid(x):
    return 1.0 / (1.0 + jnp.exp(-x))


def _silu(x):
    return x * _sigmoid(x)


def _rmsnorm(x, w):
    return x * lax.rsqrt(jnp.mean(x * x, axis=-1, keepdims=True) + EPS) * w


def _dot(a, b, **kw):
    return jnp.dot(a, b, preferred_element_type=F32, **kw)


def _in_proj_kernel(x_ref, nw_ref, wqkv_ref, wz_ref, wba_ref, wglu_ref,
                    qkv_ref, z_ref, ba_ref, glu_ref):
    hb = _rmsnorm(x_ref[...], nw_ref[...]).astype(BF16)
    qkv_ref[...] = _dot(hb, wqkv_ref[...])
    z_ref[...] = _dot(hb, wz_ref[...])
    ba_ref[...] = _dot(hb, wba_ref[...])
    glu_ref[...] = _dot(hb, wglu_ref[...])


def _in_proj(x, nw, wqkv, wz, wba, wglu, tm=256):
    T, D = x.shape
    full = lambda a: pl.BlockSpec(a.shape, lambda i: (0, 0))
    rows = lambda n: pl.BlockSpec((tm, n), lambda i: (i, 0))
    widths = (wqkv.shape[1], wz.shape[1], wba.shape[1], wglu.shape[1])
    return pl.pallas_call(
        _in_proj_kernel,
        grid=(T // tm,),
        in_specs=[rows(D), full(nw), full(wqkv), full(wz), full(wba), full(wglu)],
        out_specs=[rows(n) for n in widths],
        out_shape=[jax.ShapeDtypeStruct((T, n), F32) for n in widths],
        compiler_params=_params("parallel"),
        name="in_proj",
    )(x, nw, wqkv, wz, wba, wglu)


def _deltanet_kernel(qkv_ref, z_ref, ba_ref, cw_ref, alog_ref, dtb_ref, onorm_ref, y_ref,
                     state_ref, tail_ref, cbuf_ref, q_s, k_s, v_s, beta_s, g_s,
                     u_s, w_s, qd_s, kd_s, qk_s, gl_s):
    tb = qkv_ref.shape[0]
    n_chunks = tb // CHUNK
    halo = SUBLANES

    @pl.when(pl.program_id(1) == 0)
    def _():
        state_ref[...] = jnp.zeros_like(state_ref)
        tail_ref[...] = jnp.zeros_like(tail_ref)

    cbuf_ref[0:halo, :] = tail_ref[...]
    cbuf_ref[halo:halo + tb, :] = qkv_ref[...]
    tail_ref[...] = qkv_ref[tb - halo:tb, :]
    first = halo - (SHORT_CONV - 1)
    acc = cw_ref[0:1, :] * cbuf_ref[first:first + tb, :]
    for j in range(1, SHORT_CONV):
        acc = acc + cw_ref[j:j + 1, :] * cbuf_ref[first + j:first + j + tb, :]
    c = _silu(acc)

    for h in range(DN_HEADS):
        hs = slice(h * DN_HEAD_DIM, (h + 1) * DN_HEAD_DIM)
        q = c[:, h * DN_HEAD_DIM:(h + 1) * DN_HEAD_DIM]
        k = c[:, DN_WIDTH + h * DN_HEAD_DIM:DN_WIDTH + (h + 1) * DN_HEAD_DIM]
        q_s[:, hs] = q * lax.rsqrt(jnp.sum(q * q, axis=-1, keepdims=True) + EPS) * (DN_HEAD_DIM ** -0.5)
        k_s[:, hs] = k * lax.rsqrt(jnp.sum(k * k, axis=-1, keepdims=True) + EPS)
    v_s[...] = c[:, 2 * DN_WIDTH:3 * DN_WIDTH]

    ba = ba_ref[...]
    beta_s[...] = _sigmoid(ba)
    xa = ba + dtb_ref[...]
    softplus = jnp.maximum(xa, 0.0) + jnp.log1p(jnp.exp(-jnp.abs(xa)))
    g_s[...] = -jnp.exp(alog_ref[...]) * softplus

    row = lax.broadcasted_iota(jnp.int32, (CHUNK, CHUNK), 0)
    col = lax.broadcasted_iota(jnp.int32, (CHUNK, CHUNK), 1)
    incl = row >= col
    strict = row > col
    tril = incl.astype(F32)
    eye = (row == col).astype(F32)
    sel = (lax.broadcasted_iota(jnp.int32, (SUBLANES, LANES), 0)
           == lax.broadcasted_iota(jnp.int32, (SUBLANES, LANES), 1)).astype(F32)

    def intra_chunk(ci, carry):
        r0 = pl.multiple_of(ci * CHUNK, CHUNK)
        rows = pl.ds(r0, CHUNK)
        gc = _dot(tril, g_s[rows, :], precision=HIGHEST)
        gct = lax.dot_general(sel, gc, NT_DIMS, precision=HIGHEST,
                              preferred_element_type=F32)
        gl_s[pl.ds(ci, 1), :] = jnp.exp(gc[CHUNK - 1:CHUNK, :])
        for h in range(DN_HEADS):
            hs = slice(h * DN_HEAD_DIM, (h + 1) * DN_HEAD_DIM)
            lane = DECAY_LANE0 + h
            gcol = gc[:, lane:lane + 1]
            grow = gct[lane:lane + 1, :]
            glast = gc[CHUNK - 1:CHUNK, lane:lane + 1]
            decay = jnp.where(incl, jnp.exp(jnp.where(incl, gcol - grow, 0.0)), 0.0)
            q = q_s[rows, hs]
            k = k_s[rows, hs]
            v = v_s[rows, hs]
            beta = beta_s[rows, BETA_LANE0 + h:BETA_LANE0 + h + 1]
            kk = lax.dot_general(k, k, NT_DIMS, preferred_element_type=F32)
            a = jnp.where(strict, beta * kk * decay, 0.0)
            eg = jnp.exp(gcol)
            rhs = jnp.concatenate([beta * v, (beta * eg) * k], axis=-1)
            inv = eye - a
            pw = a
            for _ in range(5):
                pw = _dot(pw, pw, precision=HIGHEST)
                inv = inv + _dot(inv, pw, precision=HIGHEST)
            sol = _dot(inv, rhs, precision=HIGHEST)
            u_s[rows, hs] = sol[:, :DN_HEAD_DIM]
            w_s[rows, hs] = sol[:, DN_HEAD_DIM:]
            qk_s[ci, h] = lax.dot_general(q, k, NT_DIMS, preferred_element_type=F32) * decay
            qd_s[rows, hs] = q * eg
            kd_s[rows, hs] = k * jnp.exp(glast - gcol)
        return carry

    lax.fori_loop(0, n_chunks, intra_chunk, 0)

    def inter_chunk(ci, carry):
        r0 = pl.multiple_of(ci * CHUNK, CHUNK)
        rows = pl.ds(r0, CHUNK)
        glrow = gl_s[pl.ds(ci, 1), :]
        for h in range(DN_HEADS):
            hs = slice(h * DN_HEAD_DIM, (h + 1) * DN_HEAD_DIM)
            lane = DECAY_LANE0 + h
            s = state_ref[h]
            v_new = u_s[rows, hs] - _dot(w_s[rows, hs], s)
            o = _dot(qd_s[rows, hs], s) + _dot(qk_s[ci, h], v_new)
            state_ref[h] = s * glrow[:, lane:lane + 1] + lax.dot_general(
                kd_s[rows, hs], v_new, TN_DIMS, preferred_element_type=F32)
            o = o * lax.rsqrt(jnp.mean(o * o, axis=-1, keepdims=True) + EPS) * onorm_ref[...]
            y_ref[rows, hs] = (o * _silu(z_ref[rows, hs])).astype(y_ref.dtype)
        return carry

    lax.fori_loop(0, n_chunks, inter_chunk, 0)


def _deltanet(qkv, z, ba, cw, alog, dtb, onorm, batch, tb=512):
    T = qkv.shape[0]
    nb = T // batch // tb
    n_chunks = tb // CHUNK
    full = lambda a: pl.BlockSpec(a.shape, lambda b, i: (0, 0))
    rows = lambda n: pl.BlockSpec((tb, n), lambda b, i: (b * nb + i, 0))
    scr = lambda *s: pltpu.VMEM(s, F32)
    return pl.pallas_call(
        _deltanet_kernel,
        grid=(batch, nb),
        in_specs=[rows(3 * DN_WIDTH), rows(DN_WIDTH), rows(LANES),
                  full(cw), full(alog), full(dtb), full(onorm)],
        out_specs=rows(DN_WIDTH),
        out_shape=jax.ShapeDtypeStruct((T, DN_WIDTH), BF16),
        scratch_shapes=[
            scr(DN_HEADS, DN_HEAD_DIM, DN_HEAD_DIM),
            scr(SUBLANES, 3 * DN_WIDTH),
            scr(SUBLANES + tb, 3 * DN_WIDTH),
            scr(tb, DN_WIDTH), scr(tb, DN_WIDTH), scr(tb, DN_WIDTH),
            scr(tb, LANES), scr(tb, LANES),
            scr(tb, DN_WIDTH), scr(tb, DN_WIDTH),
            scr(tb, DN_WIDTH), scr(tb, DN_WIDTH),
            scr(n_chunks, DN_HEADS, CHUNK, CHUNK),
            scr(n_chunks, LANES),
        ],
        compiler_params=_params("parallel", "arbitrary"),
        name="deltanet",
    )(qkv, z, ba, cw, alog, dtb, onorm)


def _conformer_kernel(glu_ref, wdw_ref, bdw_ref, lng_ref, lnb_ref, y_ref, cbuf_ref):
    tc = glu_ref.shape[0]
    halo = 4 * SUBLANES

    @pl.when(pl.program_id(1) == 0)
    def _():
        cbuf_ref[0:halo, :] = jnp.zeros((halo, CONF_WIDTH), F32)

    u = glu_ref[...]
    g = u[:, :CONF_WIDTH] * _sigmoid(u[:, CONF_WIDTH:])
    cbuf_ref[halo:halo + tc, :] = g
    first = halo - (CONF_KERNEL - 1)
    acc = bdw_ref[...] + wdw_ref[0:1, :] * cbuf_ref[first:first + tc, :]
    for j in range(1, CONF_KERNEL):
        acc = acc + wdw_ref[j:j + 1, :] * cbuf_ref[first + j:first + j + tc, :]
    cbuf_ref[0:halo, :] = cbuf_ref[tc:tc + halo, :]
    mu = jnp.mean(acc, axis=-1, keepdims=True)
    d = acc - mu
    var = jnp.mean(d * d, axis=-1, keepdims=True)
    y = d * lax.rsqrt(var + EPS) * lng_ref[...] + lnb_ref[...]
    y_ref[...] = _silu(y).astype(y_ref.dtype)


def _conformer(glu, wdw, bdw, lng, lnb, batch, tc=512):
    T = glu.shape[0]
    nb = T // batch // tc
    full = lambda a: pl.BlockSpec(a.shape, lambda b, i: (0, 0))
    rows = lambda n: pl.BlockSpec((tc, n), lambda b, i: (b * nb + i, 0))
    return pl.pallas_call(
        _conformer_kernel,
        grid=(batch, nb),
        in_specs=[rows(2 * CONF_WIDTH), full(wdw), full(bdw), full(lng), full(lnb)],
        out_specs=rows(CONF_WIDTH),
        out_shape=jax.ShapeDtypeStruct((T, CONF_WIDTH), BF16),
        scratch_shapes=[pltpu.VMEM((4 * SUBLANES + tc, CONF_WIDTH), F32)],
        compiler_params=_params("parallel", "arbitrary"),
        name="conformer",
    )(glu, wdw, bdw, lng, lnb)


ROUTE_GATE0, ROUTE_GATE1, ROUTE_EXPERT0, ROUTE_EXPERT1, ROUTE_RANK0, ROUTE_RANK1 = range(6)


def _lane_argmax(vals, valid, lane):
    neg = jnp.float32(-1e30)
    m = jnp.max(jnp.where(valid, vals, neg), axis=-1, keepdims=True)
    idx = jnp.min(jnp.where(valid & (vals == m), lane, LANES), axis=-1, keepdims=True)
    return m, idx


def _out_router_kernel(x_ref, ydn_ref, ycf_ref, wo1_ref, wo2_ref, nw_ref, wr_ref, br_ref,
                       x2_ref, h_ref, route_ref, cnt_ref, base_ref):
    tm = x_ref.shape[0]

    @pl.when(pl.program_id(0) == 0)
    def _():
        base_ref[...] = jnp.zeros_like(base_ref)

    x2 = x_ref[...] + _dot(ydn_ref[...], wo1_ref[...]) + _dot(ycf_ref[...], wo2_ref[...])
    x2_ref[...] = x2
    h = _rmsnorm(x2, nw_ref[...])
    h_ref[...] = h
    logits = _dot(h, wr_ref[...], precision=HIGHEST) + br_ref[...]

    lane = lax.broadcasted_iota(jnp.int32, (tm, LANES), 1)
    gvalid = (lane >= GROUP_LANE0) & (lane < GROUP_LANE0 + N_GROUPS)
    gmax, _ = _lane_argmax(logits, gvalid, lane)
    gexp = jnp.where(gvalid, jnp.exp(jnp.where(gvalid, logits - gmax, 0.0)), 0.0)
    p_grp = gexp / jnp.sum(gexp, axis=-1, keepdims=True)
    p_g, g_lane = _lane_argmax(p_grp, gvalid, lane)
    g_sel = g_lane - GROUP_LANE0
    evalid = (lane < N_EXPERTS) & ((lane // EXPERTS_PER_GROUP) == g_sel)
    emax, _ = _lane_argmax(logits, evalid, lane)
    eexp = jnp.where(evalid, jnp.exp(jnp.where(evalid, logits - emax, 0.0)), 0.0)
    p_e = eexp / jnp.sum(eexp, axis=-1, keepdims=True)
    p1, e1 = _lane_argmax(p_e, evalid, lane)
    p2, e2 = _lane_argmax(p_e, evalid & (lane != e1), lane)
    psum = p1 + p2
    gate1 = p_g * p1 / psum
    gate2 = p_g * p2 / psum

    hot1 = (lane == e1).astype(F32)
    hot2 = (lane == e2).astype(F32)
    hot = hot1 + hot2
    trow = lax.broadcasted_iota(jnp.int32, (tm, tm), 0)
    tcol = lax.broadcasted_iota(jnp.int32, (tm, tm), 1)
    before = (trow > tcol).astype(BF16)
    seen = base_ref[...] + _dot(before, hot.astype(BF16))
    rank1 = jnp.sum(hot1 * seen, axis=-1, keepdims=True)
    rank2 = jnp.sum(hot2 * seen, axis=-1, keepdims=True)
    base_ref[...] = base_ref[...] + jnp.sum(hot, axis=0, keepdims=True)
    cnt_ref[...] = base_ref[...]

    route = jnp.zeros((tm, LANES), F32)
    for pos, val in ((ROUTE_GATE0, gate1), (ROUTE_GATE1, gate2),
                     (ROUTE_EXPERT0, e1.astype(F32)), (ROUTE_EXPERT1, e2.astype(F32)),
                     (ROUTE_RANK0, rank1), (ROUTE_RANK1, rank2)):
        route = jnp.where(lane == pos, val, route)
    route_ref[...] = route


def _out_router(x, ydn, ycf, wo1, wo2, nw, wr, br, tm=256):
    T, D = x.shape
    full = lambda a: pl.BlockSpec(a.shape, lambda i: (0, 0))
    rows = lambda n: pl.BlockSpec((tm, n), lambda i: (i, 0))
    return pl.pallas_call(
        _out_router_kernel,
        grid=(T // tm,),
        in_specs=[rows(D), rows(DN_WIDTH), rows(CONF_WIDTH), full(wo1), full(wo2), full(nw),
                  full(wr), full(br)],
        out_specs=[rows(D), rows(D), rows(LANES), pl.BlockSpec((1, LANES), lambda i: (0, 0))],
        out_shape=[jax.ShapeDtypeStruct((T, D), F32), jax.ShapeDtypeStruct((T, D), F32),
                   jax.ShapeDtypeStruct((T, LANES), F32), jax.ShapeDtypeStruct((1, LANES), F32)],
        scratch_shapes=[pltpu.VMEM((1, LANES), F32)],
        compiler_params=_params("arbitrary"),
        name="out_router",
    )(x, ydn, ycf, wo1, wo2, nw, wr, br)


def _dispatch_kernel(dest_ref, zb_ref, h_ref, xs_ref, zero_ref, row_sem, zero_sem):
    tm = h_ref.shape[0]
    base = pl.program_id(0) * tm

    def zero_copy(e):
        start = pl.multiple_of(zb_ref[e], SLOT_BLOCK)
        return pltpu.make_async_copy(zero_ref, xs_ref.at[pl.ds(start, SLOT_BLOCK)], zero_sem)

    @pl.when(pl.program_id(0) == 0)
    def _():
        zero_ref[...] = jnp.zeros_like(zero_ref)

        def start(e, carry):
            @pl.when(zb_ref[e] >= 0)
            def _():
                zero_copy(e).start()
            return carry

        def wait(e, carry):
            @pl.when(zb_ref[e] >= 0)
            def _():
                zero_copy(e).wait()
            return carry

        lax.fori_loop(0, 2 * N_EXPERTS, start, 0)
        lax.fori_loop(0, 2 * N_EXPERTS, wait, 0)

    def row_copy(r, slot):
        return pltpu.make_async_copy(h_ref.at[pl.ds(r, 1)], xs_ref.at[pl.ds(slot, 1)], row_sem)

    def start(r, carry):
        row_copy(r, dest_ref[2 * (base + r)]).start()
        row_copy(r, dest_ref[2 * (base + r) + 1]).start()
        return carry

    def wait(r, carry):
        row_copy(r, dest_ref[2 * (base + r)]).wait()
        row_copy(r, dest_ref[2 * (base + r) + 1]).wait()
        return carry

    lax.fori_loop(0, tm, start, 0)
    lax.fori_loop(0, tm, wait, 0)


def _dispatch(dest, zero_blocks, h, n_slots, tm=256):
    T, D = h.shape
    return pl.pallas_call(
        _dispatch_kernel,
        grid_spec=pltpu.PrefetchScalarGridSpec(
            num_scalar_prefetch=2,
            grid=(T // tm,),
            in_specs=[pl.BlockSpec((tm, D), lambda i, *_: (i, 0))],
            out_specs=pl.BlockSpec(memory_space=pl.ANY),
            scratch_shapes=[pltpu.VMEM((SLOT_BLOCK, D), F32),
                            pltpu.SemaphoreType.DMA(()), pltpu.SemaphoreType.DMA(())],
        ),
        out_shape=jax.ShapeDtypeStruct((n_slots, D), F32),
        compiler_params=_params("arbitrary"),
        name="dispatch",
    )(dest, zero_blocks, h)


def _experts_kernel(be_ref, nused_ref, xs_ref, wg_ref, wu_ref, wd_ref, ys_ref,
                    wg_s, wu_s, wd_s):
    i = pl.program_id(0)

    @pl.when(i < nused_ref[0])
    def _():
        changed = jnp.logical_or(i == 0, be_ref[i] != be_ref[jnp.maximum(i - 1, 0)])

        @pl.when(changed)
        def _():
            wg_s[...] = wg_ref[0].astype(BF16)
            wu_s[...] = wu_ref[0].astype(BF16)
            wd_s[...] = wd_ref[0].astype(BF16)

        xb = xs_ref[...].astype(BF16)
        act = _silu(_dot(xb, wg_s[...])) * _dot(xb, wu_s[...])
        ys_ref[...] = _dot(act.astype(BF16), wd_s[...])

    @pl.when(i >= nused_ref[0])
    def _():
        ys_ref[...] = jnp.zeros_like(ys_ref)


def _experts(block_expert, n_used, xs, w_gate, w_up, w_down):
    P, D = xs.shape
    nb = P // SLOT_BLOCK
    slot = lambda i, be, nu: (jnp.minimum(i, nu[0] - 1), 0)
    wsel = lambda i, be, nu: (be[i], 0, 0)
    return pl.pallas_call(
        _experts_kernel,
        grid_spec=pltpu.PrefetchScalarGridSpec(
            num_scalar_prefetch=2,
            grid=(nb,),
            in_specs=[pl.BlockSpec((SLOT_BLOCK, D), slot),
                      pl.BlockSpec((1, D, D_EXPERT), wsel),
                      pl.BlockSpec((1, D, D_EXPERT), wsel),
                      pl.BlockSpec((1, D_EXPERT, D), wsel)],
            out_specs=pl.BlockSpec((SLOT_BLOCK, D), lambda i, be, nu: (i, 0)),
            scratch_shapes=[pltpu.VMEM((D, D_EXPERT), BF16), pltpu.VMEM((D, D_EXPERT), BF16),
                            pltpu.VMEM((D_EXPERT, D), BF16)],
        ),
        out_shape=jax.ShapeDtypeStruct((P, D), F32),
        compiler_params=_params("arbitrary"),
        name="experts",
    )(block_expert, n_used, xs, w_gate, w_up, w_down)


def _combine_kernel(dest_ref, x_ref, route_ref, nw_ref, ys_ref, o_ref, gbuf_ref, sem, *, final_norm):
    tm = x_ref.shape[0]
    base = pl.program_id(0) * tm

    def row_copy(r, k):
        return pltpu.make_async_copy(ys_ref.at[pl.ds(dest_ref[2 * (base + r) + k], 1)],
                                     gbuf_ref.at[k, pl.ds(r, 1)], sem)

    def start(r, carry):
        row_copy(r, 0).start()
        row_copy(r, 1).start()
        return carry

    def wait(r, carry):
        row_copy(r, 0).wait()
        row_copy(r, 1).wait()
        return carry

    lax.fori_loop(0, tm, start, 0)
    lax.fori_loop(0, tm, wait, 0)
    route = route_ref[...]
    g0 = route[:, ROUTE_GATE0:ROUTE_GATE0 + 1]
    g1 = route[:, ROUTE_GATE1:ROUTE_GATE1 + 1]
    x = x_ref[...] + (gbuf_ref[0] * g0 + gbuf_ref[1] * g1)
    o_ref[...] = _rmsnorm(x, nw_ref[...]) if final_norm else x


def _combine(dest, x, route, nw, ys, final_norm, tm=256):
    T, D = x.shape
    return pl.pallas_call(
        functools.partial(_combine_kernel, final_norm=final_norm),
        grid_spec=pltpu.PrefetchScalarGridSpec(
            num_scalar_prefetch=1,
            grid=(T // tm,),
            in_specs=[pl.BlockSpec((tm, D), lambda i, d: (i, 0)),
                      pl.BlockSpec((tm, LANES), lambda i, d: (i, 0)),
                      pl.BlockSpec((1, D), lambda i, d: (0, 0)),
                      pl.BlockSpec(memory_space=pl.ANY)],
            out_specs=pl.BlockSpec((tm, D), lambda i, d: (i, 0)),
            scratch_shapes=[pltpu.VMEM((2, tm, D), F32), pltpu.SemaphoreType.DMA(())],
        ),
        out_shape=jax.ShapeDtypeStruct((T, D), F32),
        compiler_params=_params("arbitrary"),
        name="combine",
    )(dest, x, route, nw, ys)


def _slot_plan(route, counts_row, n_tokens):
    n_blocks = 2 * n_tokens // SLOT_BLOCK + N_EXPERTS
    counts = counts_row[0, :N_EXPERTS].astype(jnp.int32)
    padded = (counts + SLOT_BLOCK - 1) // SLOT_BLOCK * SLOT_BLOCK
    pad_ends = jnp.cumsum(padded)
    pad_starts = pad_ends - padded
    expert = route[:, ROUTE_EXPERT0:ROUTE_EXPERT1 + 1].astype(jnp.int32)
    rank = route[:, ROUTE_RANK0:ROUTE_RANK1 + 1].astype(jnp.int32)
    dest = (pad_starts[expert] + rank).reshape(-1)
    n_used = (pad_ends[-1] // SLOT_BLOCK).astype(jnp.int32)
    blk = jnp.arange(n_blocks, dtype=jnp.int32)
    block_expert = jnp.searchsorted(pad_ends, jnp.minimum(blk, n_used - 1) * SLOT_BLOCK,
                                    side='right').astype(jnp.int32)
    block_expert = jnp.minimum(block_expert, N_EXPERTS - 1)
    last_blocks = jnp.where(counts > 0, pad_ends - SLOT_BLOCK, -1)
    tail = n_used + jnp.arange(N_EXPERTS, dtype=jnp.int32)
    tail_blocks = jnp.where(tail < n_blocks, tail * SLOT_BLOCK, -1)
    zero_blocks = jnp.concatenate([last_blocks, tail_blocks]).astype(jnp.int32)
    return dest, zero_blocks, block_expert, n_used.reshape(1), n_blocks * SLOT_BLOCK


def _lane_row(vals, lane0):
    return jnp.zeros((1, LANES), F32).at[0, lane0:lane0 + vals.shape[0]].set(vals.astype(F32))


def kernel(x, norm_mix, w_in, conv_qkv, a_log, dt_bias, o_norm, w_dw, b_dw, ln_g, ln_b,
           w_out, norm_ffn, w_rg, b_rg, w_re, b_re, w_gate, w_up, w_down, norm_final):
    batch, seq, d_model = x.shape
    T = batch * seq
    depth = w_in.shape[0]
    o_z = 3 * DN_WIDTH
    o_b = 4 * DN_WIDTH
    o_c = o_b + 2 * DN_HEADS
    xt = x.reshape(T, d_model)
    for l in range(depth):
        wi = w_in[l]
        wba = jnp.zeros((d_model, LANES), F32).at[:, :2 * DN_HEADS].set(wi[:, o_b:o_c])
        qkv, z, ba, glu = _in_proj(
            xt, norm_mix[l][None, :], wi[:, :o_z].astype(BF16), wi[:, o_z:o_b].astype(BF16),
            wba.astype(BF16), wi[:, o_c:].astype(BF16))
        y_dn = _deltanet(qkv, z, ba, conv_qkv[l], _lane_row(a_log[l], DECAY_LANE0),
                         _lane_row(dt_bias[l], DECAY_LANE0), o_norm[l][None, :], batch)
        y_cf = _conformer(glu, w_dw[l], b_dw[l][None, :], ln_g[l][None, :], ln_b[l][None, :], batch)
        wr = (jnp.zeros((d_model, LANES), F32).at[:, :N_EXPERTS].set(w_re[l])
              .at[:, GROUP_LANE0:GROUP_LANE0 + N_GROUPS].set(w_rg[l]))
        br = (jnp.zeros((1, LANES), F32).at[0, :N_EXPERTS].set(b_re[l])
              .at[0, GROUP_LANE0:GROUP_LANE0 + N_GROUPS].set(b_rg[l]))
        x2, h, route, counts = _out_router(
            xt, y_dn, y_cf, w_out[l][:DN_WIDTH].astype(BF16), w_out[l][DN_WIDTH:].astype(BF16),
            norm_ffn[l][None, :], wr, br)
        dest, zero_blocks, block_expert, n_used, n_slots = _slot_plan(route, counts, T)
        xs = _dispatch(dest, zero_blocks, h, n_slots)
        ys = _experts(block_expert, n_used, xs, w_gate[l], w_up[l], w_down[l])
        last = l == depth - 1
        xt = _combine(dest, x2, route, norm_final[None, :] if last else norm_ffn[l][None, :],
                      ys, final_norm=last)
    return xt.reshape(batch, seq, d_model)
```

```python
import functools

import jax
import jax.numpy as jnp
from jax import lax
from jax.experimental import pallas as pl
from jax.experimental.pallas import tpu as pltpu

EPS = 1e-6
CHUNK = 64
DN_HEADS = 4
DN_HEAD_DIM = 128
DN_WIDTH = DN_HEADS * DN_HEAD_DIM
SHORT_CONV = 4
CONF_WIDTH = 512
CONF_KERNEL = 31
N_GROUPS = 4
EXPERTS_PER_GROUP = 8
N_EXPERTS = N_GROUPS * EXPERTS_PER_GROUP
D_EXPERT = 512
LANES = 128
SUBLANES = 8
GROUP_LANE0 = N_EXPERTS
BETA_LANE0 = 0
DECAY_LANE0 = DN_HEADS
SLOT_BLOCK = 256
CHUNKS_PER_ITER = 2
VMEM_LIMIT = 56 * 1024 * 1024

F32 = jnp.float32
BF16 = jnp.bfloat16
HIGHEST = lax.Precision.HIGHEST
NT_DIMS = (((1,), (1,)), ((), ()))
TN_DIMS = (((0,), (0,)), ((), ()))


def _params(*semantics):
    return pltpu.CompilerParams(dimension_semantics=semantics, vmem_limit_bytes=VMEM_LIMIT)


def _sigmoid(x):
    return 1.0 / (1.0 + jnp.exp(-x))


def _silu(x):
    return x * _sigmoid(x)


def _rmsnorm(x, w):
    return x * lax.rsqrt(jnp.mean(x * x, axis=-1, keepdims=True) + EPS) * w


def _dot(a, b, **kw):
    return jnp.dot(a, b, preferred_element_type=F32, **kw)


def _in_proj_kernel(x_ref, nw_ref, wqkv_ref, wz_ref, wba_ref, wglu_ref,
                    qkv_ref, z_ref, ba_ref, glu_ref):
    hb = _rmsnorm(x_ref[...], nw_ref[...]).astype(BF16)
    qkv_ref[...] = _dot(hb, wqkv_ref[...])
    z_ref[...] = _dot(hb, wz_ref[...])
    ba_ref[...] = _dot(hb, wba_ref[...])
    glu_ref[...] = _dot(hb, wglu_ref[...])


def _in_proj(x, nw, wqkv, wz, wba, wglu, tm=512):
    T, D = x.shape
    full = lambda a: pl.BlockSpec(a.shape, lambda i: (0, 0), pipeline_mode=pl.Buffered(1))
    rows = lambda n: pl.BlockSpec((tm, n), lambda i: (i, 0))
    widths = (wqkv.shape[1], wz.shape[1], wba.shape[1], wglu.shape[1])
    return pl.pallas_call(
        _in_proj_kernel,
        grid=(T // tm,),
        in_specs=[rows(D), full(nw), full(wqkv), full(wz), full(wba), full(wglu)],
        out_specs=[rows(n) for n in widths],
        out_shape=[jax.ShapeDtypeStruct((T, n), F32) for n in widths],
        compiler_params=_params("parallel"),
        name="in_proj",
    )(x, nw, wqkv, wz, wba, wglu)


def _deltanet_kernel(qkv_ref, z_ref, ba_ref, cw_ref, alog_ref, dtb_ref, onorm_ref, y_ref,
                     state_ref, tail_ref, cbuf_ref, q_s, k_s, v_s, beta_s, gc_s, gct_s, gl_s,
                     u_s, ws_s, vn_s):
    tb = qkv_ref.shape[0]
    n_chunks = tb // CHUNK
    halo = SUBLANES

    @pl.when(pl.program_id(1) == 0)
    def _():
        state_ref[...] = jnp.zeros_like(state_ref)
        tail_ref[...] = jnp.zeros_like(tail_ref)

    cbuf_ref[0:halo, :] = tail_ref[...]
    cbuf_ref[halo:halo + tb, :] = qkv_ref[...]
    tail_ref[...] = qkv_ref[tb - halo:tb, :]
    first = halo - (SHORT_CONV - 1)
    acc = cw_ref[0:1, :] * cbuf_ref[first:first + tb, :]
    for j in range(1, SHORT_CONV):
        acc = acc + cw_ref[j:j + 1, :] * cbuf_ref[first + j:first + j + tb, :]
    c = _silu(acc)

    for h in range(DN_HEADS):
        hs = slice(h * DN_HEAD_DIM, (h + 1) * DN_HEAD_DIM)
        q = c[:, h * DN_HEAD_DIM:(h + 1) * DN_HEAD_DIM]
        k = c[:, DN_WIDTH + h * DN_HEAD_DIM:DN_WIDTH + (h + 1) * DN_HEAD_DIM]
        q_s[:, hs] = q * lax.rsqrt(jnp.sum(q * q, axis=-1, keepdims=True) + EPS) * (DN_HEAD_DIM ** -0.5)
        k_s[:, hs] = k * lax.rsqrt(jnp.sum(k * k, axis=-1, keepdims=True) + EPS)
    v_s[...] = c[:, 2 * DN_WIDTH:3 * DN_WIDTH]

    ba = ba_ref[...]
    beta_s[...] = _sigmoid(ba)
    xa = ba + dtb_ref[...]
    softplus = jnp.maximum(xa, 0.0) + jnp.log1p(jnp.exp(-jnp.abs(xa)))
    g = -jnp.exp(alog_ref[...]) * softplus

    pair = 2 * DN_HEAD_DIM
    row = lax.broadcasted_iota(jnp.int32, (CHUNK, LANES), 0)
    lane = lax.broadcasted_iota(jnp.int32, (CHUNK, LANES), 1)
    second = lane >= CHUNK
    col = jnp.where(second, lane - CHUNK, lane)
    incl = row >= col
    strict = row > col
    eye = (row == col).astype(F32)
    trow = lax.broadcasted_iota(jnp.int32, (CHUNK, CHUNK), 0)
    tcol = lax.broadcasted_iota(jnp.int32, (CHUNK, CHUNK), 1)
    tril = (trow >= tcol).astype(F32)
    sel = (lax.broadcasted_iota(jnp.int32, (SUBLANES, LANES), 0)
           == lax.broadcasted_iota(jnp.int32, (SUBLANES, LANES), 1)).astype(F32)

    for ci in range(n_chunks):
        gc = _dot(tril, g[ci * CHUNK:(ci + 1) * CHUNK, :], precision=HIGHEST)
        gc_s[ci * CHUNK:(ci + 1) * CHUNK, :] = gc
        nxt = pltpu.roll(gc, LANES - 1, 1)
        gct_s[ci] = lax.dot_general(sel, jnp.concatenate([gc, nxt], axis=0), NT_DIMS,
                                    precision=HIGHEST, preferred_element_type=F32)
        gl_s[ci:ci + 1, :] = jnp.exp(gc[CHUNK - 1:CHUNK, :])

    def block_diag(m):
        half = m.shape[1] // 2
        z = jnp.zeros_like(m[:, :half])
        return jnp.concatenate([jnp.concatenate([m[:, :half], z], axis=1),
                                jnp.concatenate([z, m[:, half:]], axis=1)], axis=0)

    def block_diag_lanes(m):
        return jnp.concatenate([jnp.where(second, 0.0, m), jnp.where(second, m, 0.0)], axis=0)

    def intra_chunk(it, carry):
        streams = [(it * CHUNKS_PER_ITER + sub, p) for sub in range(CHUNKS_PER_ITER)
                   for p in range(DN_HEADS // 2)]
        st = []
        for ci, p in streams:
            rows = pl.ds(pl.multiple_of(ci * CHUNK, CHUNK), CHUNK)
            ps = slice(p * pair, (p + 1) * pair)
            gc = gc_s[rows, :]
            l0 = DECAY_LANE0 + 2 * p
            gcol0, gcol1 = gc[:, l0:l0 + 1], gc[:, l0 + 1:l0 + 2]
            gcol = jnp.where(second, gcol1, gcol0)
            grow = gct_s[ci][l0:l0 + 1, :]
            decay = jnp.where(incl, jnp.exp(jnp.where(incl, gcol - grow, 0.0)), 0.0)
            q, k, v = q_s[rows, ps], k_s[rows, ps], v_s[rows, ps]
            b0 = beta_s[rows, BETA_LANE0 + 2 * p:BETA_LANE0 + 2 * p + 1]
            b1 = beta_s[rows, BETA_LANE0 + 2 * p + 1:BETA_LANE0 + 2 * p + 2]
            kkqk = lax.dot_general(jnp.concatenate([k, q], axis=0).astype(BF16),
                                   block_diag(k.astype(BF16)), NT_DIMS, preferred_element_type=F32)
            st.append(dict(ci=ci, p=p, rows=rows, ps=ps, gc=gc, l0=l0, gcol0=gcol0, gcol1=gcol1,
                           decay=decay, q=q, k=k, v=v, b0=b0, b1=b1, kkqk=kkqk))
        for s in st:
            a = jnp.where(strict, jnp.where(second, s['b1'], s['b0']) * s['kkqk'][:CHUNK] * s['decay'], 0.0)
            s['inv'] = eye - a
            s['pw'] = _dot(a.astype(BF16), block_diag_lanes(a).astype(BF16))
        for _ in range(4):
            for s in st:
                r = _dot(jnp.concatenate([s['inv'], s['pw']], axis=0).astype(BF16),
                         block_diag_lanes(s['pw']).astype(BF16))
                s['inv'] = s['inv'] + r[:CHUNK]
                s['pw'] = r[CHUNK:]
        for s in st:
            s['inv'] = s['inv'] + _dot(s['inv'].astype(BF16), block_diag_lanes(s['pw']).astype(BF16))
        for s in st:
            eg0, eg1 = jnp.exp(s['gcol0']), jnp.exp(s['gcol1'])
            k, v, q = s['k'], s['v'], s['q']
            hd = DN_HEAD_DIM
            rhs = jnp.concatenate([s['b0'] * v[:, :hd], (s['b0'] * eg0) * k[:, :hd],
                                   s['b1'] * v[:, hd:], (s['b1'] * eg1) * k[:, hd:]], axis=1)
            sol = _dot(s['inv'].astype(BF16), block_diag(rhs.astype(BF16)))
            ci, p = s['ci'], s['p']
            u_s[s['rows'], s['ps']] = jnp.concatenate([sol[:, :hd], sol[:, 2 * hd:3 * hd]], axis=1)
            ws_s[ci, p, 0:CHUNK, :] = jnp.concatenate(
                [sol[:, hd:2 * hd], sol[:, 3 * hd:]], axis=1).astype(BF16)
            ws_s[ci, p, CHUNK:2 * CHUNK, :] = jnp.concatenate(
                [q[:, :hd] * eg0, q[:, hd:] * eg1], axis=1).astype(BF16)
            vn_s[ci, p, 0:CHUNK, :] = (s['kkqk'][CHUNK:] * s['decay']).astype(BF16)
            gc, l0 = s['gc'], s['l0']
            glast0 = gc[CHUNK - 1:CHUNK, l0:l0 + 1]
            glast1 = gc[CHUNK - 1:CHUNK, l0 + 1:l0 + 2]
            kdec = jnp.concatenate([k[:, :hd] * jnp.exp(glast0 - s['gcol0']),
                                    k[:, hd:] * jnp.exp(glast1 - s['gcol1'])], axis=0)
            vn_s[ci, p, CHUNK:CHUNK + hd, :] = jnp.transpose(kdec).astype(BF16)
        return carry

    lax.fori_loop(0, n_chunks // CHUNKS_PER_ITER, intra_chunk, 0)

    def inter_chunk(ci, carry):
        rows = pl.ds(pl.multiple_of(ci * CHUNK, CHUNK), CHUNK)
        glrow = gl_s[pl.ds(ci, 1), :]
        hd = DN_HEAD_DIM
        pairs = range(DN_HEADS // 2)
        r1 = [_dot(ws_s[ci, p], block_diag(jnp.concatenate(
            [state_ref[2 * p], state_ref[2 * p + 1]], axis=1).astype(BF16))) for p in pairs]
        v_new = [u_s[rows, p * pair:(p + 1) * pair] - r1[p][:CHUNK] for p in pairs]
        r2 = [_dot(vn_s[ci, p], block_diag(v_new[p].astype(BF16))) for p in pairs]
        for p in pairs:
            o2 = r1[p][CHUNK:] + r2[p][:CHUNK]
            for j in range(2):
                h = 2 * p + j
                hs = slice(h * hd, (h + 1) * hd)
                lane_h = DECAY_LANE0 + h
                state_ref[h] = (state_ref[h] * glrow[:, lane_h:lane_h + 1]
                                + r2[p][CHUNK:, j * hd:(j + 1) * hd])
                o = o2[:, j * hd:(j + 1) * hd]
                o = o * lax.rsqrt(jnp.mean(o * o, axis=-1, keepdims=True) + EPS) * onorm_ref[...]
                y_ref[rows, hs] = (o * _silu(z_ref[rows, hs])).astype(y_ref.dtype)
        return carry

    lax.fori_loop(0, n_chunks, inter_chunk, 0)


def _deltanet(qkv, z, ba, cw, alog, dtb, onorm, batch, tb=512):
    T = qkv.shape[0]
    nb = T // batch // tb
    n_chunks = tb // CHUNK
    full = lambda a: pl.BlockSpec(a.shape, lambda b, i: (0, 0))
    rows = lambda n: pl.BlockSpec((tb, n), lambda b, i: (b * nb + i, 0))
    scr = lambda *s: pltpu.VMEM(s, F32)
    return pl.pallas_call(
        _deltanet_kernel,
        grid=(batch, nb),
        in_specs=[rows(3 * DN_WIDTH), rows(DN_WIDTH), rows(LANES),
                  full(cw), full(alog), full(dtb), full(onorm)],
        out_specs=rows(DN_WIDTH),
        out_shape=jax.ShapeDtypeStruct((T, DN_WIDTH), BF16),
        scratch_shapes=[
            scr(DN_HEADS, DN_HEAD_DIM, DN_HEAD_DIM),
            scr(SUBLANES, 3 * DN_WIDTH),
            scr(SUBLANES + tb, 3 * DN_WIDTH),
            scr(tb, DN_WIDTH), scr(tb, DN_WIDTH), scr(tb, DN_WIDTH),
            scr(tb, LANES),
            scr(tb, LANES), scr(n_chunks, SUBLANES, 2 * CHUNK), scr(n_chunks, LANES),
            scr(tb, DN_WIDTH),
            pltpu.VMEM((n_chunks, DN_HEADS // 2, 2 * CHUNK, 2 * DN_HEAD_DIM), BF16),
            pltpu.VMEM((n_chunks, DN_HEADS // 2, CHUNK + DN_HEAD_DIM, 2 * CHUNK), BF16),
        ],
        compiler_params=_params("parallel", "arbitrary"),
        name="deltanet",
    )(qkv, z, ba, cw, alog, dtb, onorm)


def _conformer_kernel(glu_ref, wdw_ref, bdw_ref, lng_ref, lnb_ref, y_ref, cbuf_ref):
    tc = glu_ref.shape[0]
    halo = 4 * SUBLANES

    @pl.when(pl.program_id(1) == 0)
    def _():
        cbuf_ref[0:halo, :] = jnp.zeros((halo, CONF_WIDTH), F32)

    u = glu_ref[...]
    g = u[:, :CONF_WIDTH] * _sigmoid(u[:, CONF_WIDTH:])
    cbuf_ref[halo:halo + tc, :] = g
    first = halo - (CONF_KERNEL - 1)
    acc = bdw_ref[...]
    for phase in range(SUBLANES):
        taps = [j for j in range(CONF_KERNEL) if (first + j) % SUBLANES == phase]
        n = tc if phase == 0 else tc + SUBLANES
        part = None
        for j in taps:
            r0 = first + j - phase
            term = wdw_ref[j:j + 1, :] * cbuf_ref[r0:r0 + n, :]
            part = term if part is None else part + term
        acc = acc + part[phase:phase + tc, :]
    cbuf_ref[0:halo, :] = cbuf_ref[tc:tc + halo, :]
    mu = jnp.mean(acc, axis=-1, keepdims=True)
    d = acc - mu
    var = jnp.mean(d * d, axis=-1, keepdims=True)
    y = d * lax.rsqrt(var + EPS) * lng_ref[...] + lnb_ref[...]
    y_ref[...] = _silu(y).astype(y_ref.dtype)


def _conformer(glu, wdw, bdw, lng, lnb, batch, tc=512):
    T = glu.shape[0]
    nb = T // batch // tc
    full = lambda a: pl.BlockSpec(a.shape, lambda b, i: (0, 0))
    rows = lambda n: pl.BlockSpec((tc, n), lambda b, i: (b * nb + i, 0))
    return pl.pallas_call(
        _conformer_kernel,
        grid=(batch, nb),
        in_specs=[rows(2 * CONF_WIDTH), full(wdw), full(bdw), full(lng), full(lnb)],
        out_specs=rows(CONF_WIDTH),
        out_shape=jax.ShapeDtypeStruct((T, CONF_WIDTH), BF16),
        scratch_shapes=[pltpu.VMEM((4 * SUBLANES + tc, CONF_WIDTH), F32)],
        compiler_params=_params("parallel", "arbitrary"),
        name="conformer",
    )(glu, wdw, bdw, lng, lnb)


ROUTE_GATE0, ROUTE_GATE1, ROUTE_EXPERT0, ROUTE_EXPERT1, ROUTE_RANK0, ROUTE_RANK1 = range(6)


def _lane_argmax(vals, valid, lane):
    neg = jnp.float32(-1e30)
    m = jnp.max(jnp.where(valid, vals, neg), axis=-1, keepdims=True)
    idx = jnp.min(jnp.where(valid & (vals == m), lane, LANES), axis=-1, keepdims=True)
    return m, idx


def _out_router_kernel(x_ref, ydn_ref, ycf_ref, wo1_ref, wo2_ref, nw_ref, wrh_ref, wrl_ref, br_ref,
                       x2_ref, h_ref, route_ref, cnt_ref, base_ref):
    tm = x_ref.shape[0]

    @pl.when(pl.program_id(0) == 0)
    def _():
        base_ref[...] = jnp.zeros_like(base_ref)

    x2 = x_ref[...] + _dot(ydn_ref[...], wo1_ref[...]) + _dot(ycf_ref[...], wo2_ref[...])
    x2_ref[...] = x2
    h = _rmsnorm(x2, nw_ref[...])
    h_ref[...] = h
    h_hi = h.astype(BF16)
    h_lo = (h - h_hi.astype(F32)).astype(BF16)
    hh = _dot(jnp.concatenate([h_hi, h_lo], axis=0), wrh_ref[...])
    logits = hh[:tm] + hh[tm:] + _dot(h_hi, wrl_ref[...]) + br_ref[...]

    lane = lax.broadcasted_iota(jnp.int32, (tm, LANES), 1)
    gvalid = (lane >= GROUP_LANE0) & (lane < GROUP_LANE0 + N_GROUPS)
    gmax, _ = _lane_argmax(logits, gvalid, lane)
    gexp = jnp.where(gvalid, jnp.exp(jnp.where(gvalid, logits - gmax, 0.0)), 0.0)
    p_grp = gexp / jnp.sum(gexp, axis=-1, keepdims=True)
    p_g, g_lane = _lane_argmax(p_grp, gvalid, lane)
    g_sel = g_lane - GROUP_LANE0
    evalid = (lane < N_EXPERTS) & ((lane // EXPERTS_PER_GROUP) == g_sel)
    emax, _ = _lane_argmax(logits, evalid, lane)
    eexp = jnp.where(evalid, jnp.exp(jnp.where(evalid, logits - emax, 0.0)), 0.0)
    p_e = eexp / jnp.sum(eexp, axis=-1, keepdims=True)
    p1, e1 = _lane_argmax(p_e, evalid, lane)
    p2, e2 = _lane_argmax(p_e, evalid & (lane != e1), lane)
    psum = p1 + p2
    gate1 = p_g * p1 / psum
    gate2 = p_g * p2 / psum

    hot1 = (lane == e1).astype(F32)
    hot2 = (lane == e2).astype(F32)
    hot = hot1 + hot2
    trow = lax.broadcasted_iota(jnp.int32, (tm, tm), 0)
    tcol = lax.broadcasted_iota(jnp.int32, (tm, tm), 1)
    before = (trow > tcol).astype(BF16)
    seen = base_ref[...] + _dot(before, hot.astype(BF16))
    rank1 = jnp.sum(hot1 * seen, axis=-1, keepdims=True)
    rank2 = jnp.sum(hot2 * seen, axis=-1, keepdims=True)
    base_ref[...] = base_ref[...] + jnp.sum(hot, axis=0, keepdims=True)
    cnt_ref[...] = base_ref[...]

    route = jnp.zeros((tm, LANES), F32)
    for pos, val in ((ROUTE_GATE0, gate1), (ROUTE_GATE1, gate2),
                     (ROUTE_EXPERT0, e1.astype(F32)), (ROUTE_EXPERT1, e2.astype(F32)),
                     (ROUTE_RANK0, rank1), (ROUTE_RANK1, rank2)):
        route = jnp.where(lane == pos, val, route)
    route_ref[...] = route


def _out_router(x, ydn, ycf, wo1, wo2, nw, wr, br, tm=512):
    T, D = x.shape
    full = lambda a: pl.BlockSpec(a.shape, lambda i: (0, 0), pipeline_mode=pl.Buffered(1))
    rows = lambda n: pl.BlockSpec((tm, n), lambda i: (i, 0))
    wr_hi = wr.astype(BF16)
    wr_lo = (wr - wr_hi.astype(F32)).astype(BF16)
    return pl.pallas_call(
        _out_router_kernel,
        grid=(T // tm,),
        in_specs=[rows(D), rows(DN_WIDTH), rows(CONF_WIDTH), full(wo1), full(wo2), full(nw),
                  full(wr_hi), full(wr_lo), full(br)],
        out_specs=[rows(D), rows(D), rows(LANES), pl.BlockSpec((1, LANES), lambda i: (0, 0))],
        out_shape=[jax.ShapeDtypeStruct((T, D), F32), jax.ShapeDtypeStruct((T, D), F32),
                   jax.ShapeDtypeStruct((T, LANES), F32), jax.ShapeDtypeStruct((1, LANES), F32)],
        scratch_shapes=[pltpu.VMEM((1, LANES), F32)],
        compiler_params=_params("arbitrary"),
        name="out_router",
    )(x, ydn, ycf, wo1, wo2, nw, wr_hi, wr_lo, br)


def _dispatch_kernel(dest_ref, zb_ref, h_ref, xs_ref, zero_ref, row_sem, zero_sem):
    tm = h_ref.shape[0]
    base = pl.program_id(0) * tm

    def zero_copy(e):
        start = pl.multiple_of(zb_ref[e], SLOT_BLOCK)
        return pltpu.make_async_copy(zero_ref, xs_ref.at[pl.ds(start, SLOT_BLOCK)], zero_sem)

    @pl.when(pl.program_id(0) == 0)
    def _():
        zero_ref[...] = jnp.zeros_like(zero_ref)

        def start(e, carry):
            @pl.when(zb_ref[e] >= 0)
            def _():
                zero_copy(e).start()
            return carry

        def wait(e, carry):
            @pl.when(zb_ref[e] >= 0)
            def _():
                zero_copy(e).wait()
            return carry

        lax.fori_loop(0, 2 * N_EXPERTS, start, 0)
        lax.fori_loop(0, 2 * N_EXPERTS, wait, 0)

    def row_copy(r, slot):
        return pltpu.make_async_copy(h_ref.at[pl.ds(r, 1)], xs_ref.at[pl.ds(slot, 1)], row_sem)

    def start(r, carry):
        row_copy(r, dest_ref[2 * (base + r)]).start()
        row_copy(r, dest_ref[2 * (base + r) + 1]).start()
        return carry

    lax.fori_loop(0, tm, start, 0)
    tile_wait = pltpu.make_async_copy(h_ref, xs_ref.at[pl.ds(0, tm)], row_sem)
    tile_wait.wait()
    tile_wait.wait()


def _dispatch(dest, zero_blocks, h, n_slots, tm=256):
    T, D = h.shape
    return pl.pallas_call(
        _dispatch_kernel,
        grid_spec=pltpu.PrefetchScalarGridSpec(
            num_scalar_prefetch=2,
            grid=(T // tm,),
            in_specs=[pl.BlockSpec((tm, D), lambda i, *_: (i, 0))],
            out_specs=pl.BlockSpec(memory_space=pl.ANY),
            scratch_shapes=[pltpu.VMEM((SLOT_BLOCK, D), F32),
                            pltpu.SemaphoreType.DMA(()), pltpu.SemaphoreType.DMA(())],
        ),
        out_shape=jax.ShapeDtypeStruct((n_slots, D), F32),
        compiler_params=_params("arbitrary"),
        name="dispatch",
    )(dest, zero_blocks, h)


def _experts_kernel(be_ref, nused_ref, xs_ref, wg_ref, wu_ref, wd_ref, ys_ref,
                    wg_s, wu_s, wd_s):
    i = pl.program_id(0)

    @pl.when(i < nused_ref[0])
    def _():
        changed = jnp.logical_or(i == 0, be_ref[i] != be_ref[jnp.maximum(i - 1, 0)])

        @pl.when(changed)
        def _():
            wg_s[...] = wg_ref[0].astype(BF16)
            wu_s[...] = wu_ref[0].astype(BF16)
            wd_s[...] = wd_ref[0].astype(BF16)

        xb = xs_ref[...].astype(BF16)
        act = _silu(_dot(xb, wg_s[...])) * _dot(xb, wu_s[...])
        ys_ref[...] = _dot(act.astype(BF16), wd_s[...])

    @pl.when(i >= nused_ref[0])
    def _():
        ys_ref[...] = jnp.zeros_like(ys_ref)


def _experts(block_expert, n_used, xs, w_gate, w_up, w_down):
    P, D = xs.shape
    nb = P // SLOT_BLOCK
    slot = lambda i, be, nu: (jnp.minimum(i, nu[0] - 1), 0)
    wsel = lambda i, be, nu: (be[i], 0, 0)
    return pl.pallas_call(
        _experts_kernel,
        grid_spec=pltpu.PrefetchScalarGridSpec(
            num_scalar_prefetch=2,
            grid=(nb,),
            in_specs=[pl.BlockSpec((SLOT_BLOCK, D), slot),
                      pl.BlockSpec((1, D, D_EXPERT), wsel),
                      pl.BlockSpec((1, D, D_EXPERT), wsel),
                      pl.BlockSpec((1, D_EXPERT, D), wsel)],
            out_specs=pl.BlockSpec((SLOT_BLOCK, D), lambda i, be, nu: (i, 0)),
            scratch_shapes=[pltpu.VMEM((D, D_EXPERT), BF16), pltpu.VMEM((D, D_EXPERT), BF16),
                            pltpu.VMEM((D_EXPERT, D), BF16)],
        ),
        out_shape=jax.ShapeDtypeStruct((P, D), F32),
        compiler_params=_params("arbitrary"),
        name="experts",
    )(block_expert, n_used, xs, w_gate, w_up, w_down)


def _combine_kernel(dest_ref, x_ref, route_ref, nw_ref, ys_ref, o_ref, gbuf_ref, sem, *, final_norm):
    tm = x_ref.shape[0]
    base = pl.program_id(0) * tm

    def row_copy(r, k):
        return pltpu.make_async_copy(ys_ref.at[pl.ds(dest_ref[2 * (base + r) + k], 1)],
                                     gbuf_ref.at[k, pl.ds(r, 1)], sem)

    def start(r, carry):
        row_copy(r, 0).start()
        row_copy(r, 1).start()
        return carry

    lax.fori_loop(0, tm, start, 0)
    for k in range(2):
        pltpu.make_async_copy(ys_ref.at[pl.ds(0, tm)], gbuf_ref.at[k], sem).wait()
    route = route_ref[...]
    g0 = route[:, ROUTE_GATE0:ROUTE_GATE0 + 1]
    g1 = route[:, ROUTE_GATE1:ROUTE_GATE1 + 1]
    x = x_ref[...] + (gbuf_ref[0] * g0 + gbuf_ref[1] * g1)
    o_ref[...] = _rmsnorm(x, nw_ref[...]) if final_norm else x


def _combine(dest, x, route, nw, ys, final_norm, tm=256):
    T, D = x.shape
    return pl.pallas_call(
        functools.partial(_combine_kernel, final_norm=final_norm),
        grid_spec=pltpu.PrefetchScalarGridSpec(
            num_scalar_prefetch=1,
            grid=(T // tm,),
            in_specs=[pl.BlockSpec((tm, D), lambda i, d: (i, 0)),
                      pl.BlockSpec((tm, LANES), lambda i, d: (i, 0)),
                      pl.BlockSpec((1, D), lambda i, d: (0, 0)),
                      pl.BlockSpec(memory_space=pl.ANY)],
            out_specs=pl.BlockSpec((tm, D), lambda i, d: (i, 0)),
            scratch_shapes=[pltpu.VMEM((2, tm, D), F32), pltpu.SemaphoreType.DMA(())],
        ),
        out_shape=jax.ShapeDtypeStruct((T, D), F32),
        compiler_params=_params("arbitrary"),
        name="combine",
    )(dest, x, route, nw, ys)


def _slot_plan(route, counts_row, n_tokens):
    n_blocks = 2 * n_tokens // SLOT_BLOCK + N_EXPERTS
    counts = counts_row[0, :N_EXPERTS].astype(jnp.int32)
    padded = (counts + SLOT_BLOCK - 1) // SLOT_BLOCK * SLOT_BLOCK
    pad_ends = jnp.cumsum(padded)
    pad_starts = pad_ends - padded
    expert = route[:, ROUTE_EXPERT0:ROUTE_EXPERT1 + 1].astype(jnp.int32)
    rank = route[:, ROUTE_RANK0:ROUTE_RANK1 + 1].astype(jnp.int32)
    dest = (pad_starts[expert] + rank).reshape(-1)
    n_used = (pad_ends[-1] // SLOT_BLOCK).astype(jnp.int32)
    blk = jnp.arange(n_blocks, dtype=jnp.int32)
    block_expert = jnp.searchsorted(pad_ends, jnp.minimum(blk, n_used - 1) * SLOT_BLOCK,
                                    side='right').astype(jnp.int32)
    block_expert = jnp.minimum(block_expert, N_EXPERTS - 1)
    last_blocks = jnp.where(counts > 0, pad_ends - SLOT_BLOCK, -1)
    tail = n_used + jnp.arange(N_EXPERTS, dtype=jnp.int32)
    tail_blocks = jnp.where(tail < n_blocks, tail * SLOT_BLOCK, -1)
    zero_blocks = jnp.concatenate([last_blocks, tail_blocks]).astype(jnp.int32)
    return dest, zero_blocks, block_expert, n_used.reshape(1), n_blocks * SLOT_BLOCK


def _lane_row(vals, lane0):
    return jnp.zeros((1, LANES), F32).at[0, lane0:lane0 + vals.shape[0]].set(vals.astype(F32))


def kernel(x, norm_mix, w_in, conv_qkv, a_log, dt_bias, o_norm, w_dw, b_dw, ln_g, ln_b,
           w_out, norm_ffn, w_rg, b_rg, w_re, b_re, w_gate, w_up, w_down, norm_final):
    batch, seq, d_model = x.shape
    T = batch * seq
    depth = w_in.shape[0]
    o_z = 3 * DN_WIDTH
    o_b = 4 * DN_WIDTH
    o_c = o_b + 2 * DN_HEADS
    xt = x.reshape(T, d_model)
    for l in range(depth):
        wi = w_in[l]
        wba = jnp.zeros((d_model, LANES), F32).at[:, :2 * DN_HEADS].set(wi[:, o_b:o_c])
        qkv, z, ba, glu = _in_proj(
            xt, norm_mix[l][None, :], wi[:, :o_z].astype(BF16), wi[:, o_z:o_b].astype(BF16),
            wba.astype(BF16), wi[:, o_c:].astype(BF16))
        y_dn = _deltanet(qkv, z, ba, conv_qkv[l], _lane_row(a_log[l], DECAY_LANE0),
                         _lane_row(dt_bias[l], DECAY_LANE0), o_norm[l][None, :], batch)
        y_cf = _conformer(glu, w_dw[l], b_dw[l][None, :], ln_g[l][None, :], ln_b[l][None, :], batch)
        wr = (jnp.zeros((d_model, LANES), F32).at[:, :N_EXPERTS].set(w_re[l])
              .at[:, GROUP_LANE0:GROUP_LANE0 + N_GROUPS].set(w_rg[l]))
        br = (jnp.zeros((1, LANES), F32).at[0, :N_EXPERTS].set(b_re[l])
              .at[0, GROUP_LANE0:GROUP_LANE0 + N_GROUPS].set(b_rg[l]))
        x2, h, route, counts = _out_router(
            xt, y_dn, y_cf, w_out[l][:DN_WIDTH].astype(BF16), w_out[l][DN_WIDTH:].astype(BF16),
            norm_ffn[l][None, :], wr, br)
        dest, zero_blocks, block_expert, n_used, n_slots = _slot_plan(route, counts, T)
        xs = _dispatch(dest, zero_blocks, h, n_slots)
        ys = _experts(block_expert, n_used, xs, w_gate[l], w_up[l], w_down[l])
        last = l == depth - 1
        xt = _combine(dest, x2, route, norm_final[None, :] if last else norm_ffn[l][None, :],
                      ys, final_norm=last)
    return xt.reshape(batch, seq, d_model)
```

```python
import functools

import jax
import jax.numpy as jnp
from jax import lax
from jax.experimental import pallas as pl
from jax.experimental.pallas import tpu as pltpu

EPS = 1e-6
CHUNK = 64
DN_HEADS = 4
DN_HEAD_DIM = 128
DN_WIDTH = DN_HEADS * DN_HEAD_DIM
SHORT_CONV = 4
CONF_WIDTH = 512
CONF_KERNEL = 31
N_GROUPS = 4
EXPERTS_PER_GROUP = 8
N_EXPERTS = N_GROUPS * EXPERTS_PER_GROUP
D_EXPERT = 512
LANES = 128
SUBLANES = 8
GROUP_LANE0 = N_EXPERTS
BETA_LANE0 = 0
DECAY_LANE0 = DN_HEADS
SLOT_BLOCK = 512
GATHER_ROWS = SLOT_BLOCK // 2
CHUNKS_PER_ITER = 2
VMEM_LIMIT = 56 * 1024 * 1024

F32 = jnp.float32
BF16 = jnp.bfloat16
HIGHEST = lax.Precision.HIGHEST
NT_DIMS = (((1,), (1,)), ((), ()))
TN_DIMS = (((0,), (0,)), ((), ()))


def _params(*semantics):
    return pltpu.CompilerParams(dimension_semantics=semantics, vmem_limit_bytes=VMEM_LIMIT)


def _sigmoid(x):
    return 1.0 / (1.0 + jnp.exp(-x))


def _silu(x):
    return x * _sigmoid(x)


def _rmsnorm(x, w):
    return x * lax.rsqrt(jnp.mean(x * x, axis=-1, keepdims=True) + EPS) * w


def _dot(a, b, **kw):
    return jnp.dot(a, b, preferred_element_type=F32, **kw)


def _in_proj_kernel(x_ref, nw_ref, wqkv_ref, wz_ref, wba_ref, wglu_ref,
                    qkv_ref, z_ref, ba_ref, glu_ref):
    hb = _rmsnorm(x_ref[...], nw_ref[...]).astype(BF16)
    qkv_ref[...] = _dot(hb, wqkv_ref[...])
    z_ref[...] = _dot(hb, wz_ref[...])
    ba_ref[...] = _dot(hb, wba_ref[...])
    glu_ref[...] = _dot(hb, wglu_ref[...])


def _in_proj(x, nw, wqkv, wz, wba, wglu, tm=512):
    T, D = x.shape
    full = lambda a: pl.BlockSpec(a.shape, lambda i: (0, 0), pipeline_mode=pl.Buffered(1))
    rows = lambda n: pl.BlockSpec((tm, n), lambda i: (i, 0))
    widths = (wqkv.shape[1], wz.shape[1], wba.shape[1], wglu.shape[1])
    return pl.pallas_call(
        _in_proj_kernel,
        grid=(T // tm,),
        in_specs=[rows(D), full(nw), full(wqkv), full(wz), full(wba), full(wglu)],
        out_specs=[rows(n) for n in widths],
        out_shape=[jax.ShapeDtypeStruct((T, n), F32) for n in widths],
        compiler_params=_params("parallel"),
        name="in_proj",
    )(x, nw, wqkv, wz, wba, wglu)


def _deltanet_kernel(qkv_ref, z_ref, ba_ref, cw_ref, alog_ref, dtb_ref, onorm_ref, y_ref,
                     state_ref, tail_ref, cbuf_ref, q_s, k_s, v_s, beta_s, gc_s, gct_s, gl_s,
                     u_s, ws_s, vn_s):
    tb = qkv_ref.shape[0]
    n_chunks = tb // CHUNK
    halo = SUBLANES

    @pl.when(pl.program_id(1) == 0)
    def _():
        state_ref[...] = jnp.zeros_like(state_ref)
        tail_ref[...] = jnp.zeros_like(tail_ref)

    cbuf_ref[0:halo, :] = tail_ref[...]
    cbuf_ref[halo:halo + tb, :] = qkv_ref[...]
    tail_ref[...] = qkv_ref[tb - halo:tb, :]
    first = halo - (SHORT_CONV - 1)
    acc = cw_ref[0:1, :] * cbuf_ref[first:first + tb, :]
    for j in range(1, SHORT_CONV):
        acc = acc + cw_ref[j:j + 1, :] * cbuf_ref[first + j:first + j + tb, :]
    c = _silu(acc)

    for h in range(DN_HEADS):
        hs = slice(h * DN_HEAD_DIM, (h + 1) * DN_HEAD_DIM)
        q = c[:, h * DN_HEAD_DIM:(h + 1) * DN_HEAD_DIM]
        k = c[:, DN_WIDTH + h * DN_HEAD_DIM:DN_WIDTH + (h + 1) * DN_HEAD_DIM]
        q_s[:, hs] = q * lax.rsqrt(jnp.sum(q * q, axis=-1, keepdims=True) + EPS) * (DN_HEAD_DIM ** -0.5)
        k_s[:, hs] = k * lax.rsqrt(jnp.sum(k * k, axis=-1, keepdims=True) + EPS)
    v_s[...] = c[:, 2 * DN_WIDTH:3 * DN_WIDTH]

    ba = ba_ref[...]
    beta_s[...] = _sigmoid(ba)
    xa = ba + dtb_ref[...]
    softplus = jnp.maximum(xa, 0.0) + jnp.log1p(jnp.exp(-jnp.abs(xa)))
    g = -jnp.exp(alog_ref[...]) * softplus

    pair = 2 * DN_HEAD_DIM
    row = lax.broadcasted_iota(jnp.int32, (CHUNK, LANES), 0)
    lane = lax.broadcasted_iota(jnp.int32, (CHUNK, LANES), 1)
    second = lane >= CHUNK
    col = jnp.where(second, lane - CHUNK, lane)
    incl = row >= col
    strict = row > col
    eye = (row == col).astype(F32)
    trow = lax.broadcasted_iota(jnp.int32, (CHUNK, CHUNK), 0)
    tcol = lax.broadcasted_iota(jnp.int32, (CHUNK, CHUNK), 1)
    tril = (trow >= tcol).astype(F32)
    sel = (lax.broadcasted_iota(jnp.int32, (SUBLANES, LANES), 0)
           == lax.broadcasted_iota(jnp.int32, (SUBLANES, LANES), 1)).astype(F32)

    for ci in range(n_chunks):
        gc = _dot(tril, g[ci * CHUNK:(ci + 1) * CHUNK, :], precision=HIGHEST)
        gc_s[ci * CHUNK:(ci + 1) * CHUNK, :] = gc
        nxt = pltpu.roll(gc, LANES - 1, 1)
        gct_s[ci] = lax.dot_general(sel, jnp.concatenate([gc, nxt], axis=0), NT_DIMS,
                                    precision=HIGHEST, preferred_element_type=F32)
        gl_s[ci:ci + 1, :] = jnp.exp(gc[CHUNK - 1:CHUNK, :])

    def block_diag(m):
        half = m.shape[1] // 2
        z = jnp.zeros_like(m[:, :half])
        return jnp.concatenate([jnp.concatenate([m[:, :half], z], axis=1),
                                jnp.concatenate([z, m[:, half:]], axis=1)], axis=0)

    def block_diag_lanes(m):
        return jnp.concatenate([jnp.where(second, 0.0, m), jnp.where(second, m, 0.0)], axis=0)

    def intra_chunk(it, carry):
        streams = [(it * CHUNKS_PER_ITER + sub, p) for sub in range(CHUNKS_PER_ITER)
                   for p in range(DN_HEADS // 2)]
        st = []
        for ci, p in streams:
            rows = pl.ds(pl.multiple_of(ci * CHUNK, CHUNK), CHUNK)
            ps = slice(p * pair, (p + 1) * pair)
            gc = gc_s[rows, :]
            l0 = DECAY_LANE0 + 2 * p
            gcol0, gcol1 = gc[:, l0:l0 + 1], gc[:, l0 + 1:l0 + 2]
            gcol = jnp.where(second, gcol1, gcol0)
            grow = gct_s[ci][l0:l0 + 1, :]
            decay = jnp.where(incl, jnp.exp(jnp.where(incl, gcol - grow, 0.0)), 0.0)
            q, k, v = q_s[rows, ps], k_s[rows, ps], v_s[rows, ps]
            b0 = beta_s[rows, BETA_LANE0 + 2 * p:BETA_LANE0 + 2 * p + 1]
            b1 = beta_s[rows, BETA_LANE0 + 2 * p + 1:BETA_LANE0 + 2 * p + 2]
            kkqk = lax.dot_general(jnp.concatenate([k, q], axis=0).astype(BF16),
                                   block_diag(k.astype(BF16)), NT_DIMS, preferred_element_type=F32)
            st.append(dict(ci=ci, p=p, rows=rows, ps=ps, gc=gc, l0=l0, gcol0=gcol0, gcol1=gcol1,
                           decay=decay, q=q, k=k, v=v, b0=b0, b1=b1, kkqk=kkqk))
        for s in st:
            a = jnp.where(strict, jnp.where(second, s['b1'], s['b0']) * s['kkqk'][:CHUNK] * s['decay'], 0.0)
            s['inv'] = eye - a
            s['pw'] = _dot(a.astype(BF16), block_diag_lanes(a).astype(BF16))
        for _ in range(4):
            for s in st:
                r = _dot(jnp.concatenate([s['inv'], s['pw']], axis=0).astype(BF16),
                         block_diag_lanes(s['pw']).astype(BF16))
                s['inv'] = s['inv'] + r[:CHUNK]
                s['pw'] = r[CHUNK:]
        for s in st:
            s['inv'] = s['inv'] + _dot(s['inv'].astype(BF16), block_diag_lanes(s['pw']).astype(BF16))
        for s in st:
            eg0, eg1 = jnp.exp(s['gcol0']), jnp.exp(s['gcol1'])
            k, v, q = s['k'], s['v'], s['q']
            hd = DN_HEAD_DIM
            rhs = jnp.concatenate([s['b0'] * v[:, :hd], (s['b0'] * eg0) * k[:, :hd],
                                   s['b1'] * v[:, hd:], (s['b1'] * eg1) * k[:, hd:]], axis=1)
            sol = _dot(s['inv'].astype(BF16), block_diag(rhs.astype(BF16)))
            ci, p = s['ci'], s['p']
            u_s[s['rows'], s['ps']] = jnp.concatenate([sol[:, :hd], sol[:, 2 * hd:3 * hd]], axis=1)
            ws_s[ci, p, 0:CHUNK, :] = jnp.concatenate(
                [sol[:, hd:2 * hd], sol[:, 3 * hd:]], axis=1).astype(BF16)
            ws_s[ci, p, CHUNK:2 * CHUNK, :] = jnp.concatenate(
                [q[:, :hd] * eg0, q[:, hd:] * eg1], axis=1).astype(BF16)
            vn_s[ci, p, 0:CHUNK, :] = (s['kkqk'][CHUNK:] * s['decay']).astype(BF16)
            gc, l0 = s['gc'], s['l0']
            glast0 = gc[CHUNK - 1:CHUNK, l0:l0 + 1]
            glast1 = gc[CHUNK - 1:CHUNK, l0 + 1:l0 + 2]
            kdec = jnp.concatenate([k[:, :hd] * jnp.exp(glast0 - s['gcol0']),
                                    k[:, hd:] * jnp.exp(glast1 - s['gcol1'])], axis=0)
            vn_s[ci, p, CHUNK:CHUNK + hd, :] = jnp.transpose(kdec).astype(BF16)
        return carry

    lax.fori_loop(0, n_chunks // CHUNKS_PER_ITER, intra_chunk, 0)

    def inter_chunk(ci, carry):
        rows = pl.ds(pl.multiple_of(ci * CHUNK, CHUNK), CHUNK)
        glrow = gl_s[pl.ds(ci, 1), :]
        hd = DN_HEAD_DIM
        pairs = range(DN_HEADS // 2)
        r1 = [_dot(ws_s[ci, p], block_diag(jnp.concatenate(
            [state_ref[2 * p], state_ref[2 * p + 1]], axis=1).astype(BF16))) for p in pairs]
        v_new = [u_s[rows, p * pair:(p + 1) * pair] - r1[p][:CHUNK] for p in pairs]
        r2 = [_dot(vn_s[ci, p], block_diag(v_new[p].astype(BF16))) for p in pairs]
        for p in pairs:
            o2 = r1[p][CHUNK:] + r2[p][:CHUNK]
            for j in range(2):
                h = 2 * p + j
                hs = slice(h * hd, (h + 1) * hd)
                lane_h = DECAY_LANE0 + h
                state_ref[h] = (state_ref[h] * glrow[:, lane_h:lane_h + 1]
                                + r2[p][CHUNK:, j * hd:(j + 1) * hd])
                o = o2[:, j * hd:(j + 1) * hd]
                o = o * lax.rsqrt(jnp.mean(o * o, axis=-1, keepdims=True) + EPS) * onorm_ref[...]
                y_ref[rows, hs] = (o * _silu(z_ref[rows, hs])).astype(y_ref.dtype)
        return carry

    lax.fori_loop(0, n_chunks, inter_chunk, 0)


def _deltanet(qkv, z, ba, cw, alog, dtb, onorm, batch, tb=512):
    T = qkv.shape[0]
    nb = T // batch // tb
    n_chunks = tb // CHUNK
    full = lambda a: pl.BlockSpec(a.shape, lambda b, i: (0, 0))
    rows = lambda n: pl.BlockSpec((tb, n), lambda b, i: (b * nb + i, 0))
    scr = lambda *s: pltpu.VMEM(s, F32)
    return pl.pallas_call(
        _deltanet_kernel,
        grid=(batch, nb),
        in_specs=[rows(3 * DN_WIDTH), rows(DN_WIDTH), rows(LANES),
                  full(cw), full(alog), full(dtb), full(onorm)],
        out_specs=rows(DN_WIDTH),
        out_shape=jax.ShapeDtypeStruct((T, DN_WIDTH), BF16),
        scratch_shapes=[
            scr(DN_HEADS, DN_HEAD_DIM, DN_HEAD_DIM),
            scr(SUBLANES, 3 * DN_WIDTH),
            scr(SUBLANES + tb, 3 * DN_WIDTH),
            scr(tb, DN_WIDTH), scr(tb, DN_WIDTH), scr(tb, DN_WIDTH),
            scr(tb, LANES),
            scr(tb, LANES), scr(n_chunks, SUBLANES, 2 * CHUNK), scr(n_chunks, LANES),
            scr(tb, DN_WIDTH),
            pltpu.VMEM((n_chunks, DN_HEADS // 2, 2 * CHUNK, 2 * DN_HEAD_DIM), BF16),
            pltpu.VMEM((n_chunks, DN_HEADS // 2, CHUNK + DN_HEAD_DIM, 2 * CHUNK), BF16),
        ],
        compiler_params=_params("parallel", "arbitrary"),
        name="deltanet",
    )(qkv, z, ba, cw, alog, dtb, onorm)


def _conformer_kernel(glu_ref, wdw_ref, bdw_ref, lng_ref, lnb_ref, y_ref, cbuf_ref):
    tc = glu_ref.shape[0]
    halo = 4 * SUBLANES

    @pl.when(pl.program_id(1) == 0)
    def _():
        cbuf_ref[0:halo, :] = jnp.zeros((halo, CONF_WIDTH), F32)

    u = glu_ref[...]
    g = u[:, :CONF_WIDTH] * _sigmoid(u[:, CONF_WIDTH:])
    cbuf_ref[halo:halo + tc, :] = g
    first = halo - (CONF_KERNEL - 1)
    acc = bdw_ref[...]
    for phase in range(SUBLANES):
        taps = [j for j in range(CONF_KERNEL) if (first + j) % SUBLANES == phase]
        n = tc if phase == 0 else tc + SUBLANES
        part = None
        for j in taps:
            r0 = first + j - phase
            term = wdw_ref[j:j + 1, :] * cbuf_ref[r0:r0 + n, :]
            part = term if part is None else part + term
        acc = acc + part[phase:phase + tc, :]
    cbuf_ref[0:halo, :] = cbuf_ref[tc:tc + halo, :]
    mu = jnp.mean(acc, axis=-1, keepdims=True)
    d = acc - mu
    var = jnp.mean(d * d, axis=-1, keepdims=True)
    y = d * lax.rsqrt(var + EPS) * lng_ref[...] + lnb_ref[...]
    y_ref[...] = _silu(y).astype(y_ref.dtype)


def _conformer(glu, wdw, bdw, lng, lnb, batch, tc=512):
    T = glu.shape[0]
    nb = T // batch // tc
    full = lambda a: pl.BlockSpec(a.shape, lambda b, i: (0, 0))
    rows = lambda n: pl.BlockSpec((tc, n), lambda b, i: (b * nb + i, 0))
    return pl.pallas_call(
        _conformer_kernel,
        grid=(batch, nb),
        in_specs=[rows(2 * CONF_WIDTH), full(wdw), full(bdw), full(lng), full(lnb)],
        out_specs=rows(CONF_WIDTH),
        out_shape=jax.ShapeDtypeStruct((T, CONF_WIDTH), BF16),
        scratch_shapes=[pltpu.VMEM((4 * SUBLANES + tc, CONF_WIDTH), F32)],
        compiler_params=_params("parallel", "arbitrary"),
        name="conformer",
    )(glu, wdw, bdw, lng, lnb)


ROUTE_GATE0, ROUTE_GATE1, ROUTE_EXPERT0, ROUTE_EXPERT1, ROUTE_RANK0, ROUTE_RANK1 = range(6)


def _lane_argmax(vals, valid, lane):
    neg = jnp.float32(-1e30)
    m = jnp.max(jnp.where(valid, vals, neg), axis=-1, keepdims=True)
    idx = jnp.min(jnp.where(valid & (vals == m), lane, LANES), axis=-1, keepdims=True)
    return m, idx


def _out_router_kernel(x_ref, ydn_ref, ycf_ref, wo1_ref, wo2_ref, nw_ref, wrh_ref, wrl_ref, br_ref,
                       x2_ref, h_ref, route_ref, plan_ref, cnt_ref, base_ref):
    tm = x_ref.shape[0]

    @pl.when(pl.program_id(0) == 0)
    def _():
        base_ref[...] = jnp.zeros_like(base_ref)

    x2 = x_ref[...] + _dot(ydn_ref[...], wo1_ref[...]) + _dot(ycf_ref[...], wo2_ref[...])
    x2_ref[...] = x2
    h = _rmsnorm(x2, nw_ref[...])
    h_ref[...] = h
    h_hi = h.astype(BF16)
    h_lo = (h - h_hi.astype(F32)).astype(BF16)
    hh = _dot(jnp.concatenate([h_hi, h_lo], axis=0), wrh_ref[...])
    logits = hh[:tm] + hh[tm:] + _dot(h_hi, wrl_ref[...]) + br_ref[...]

    lane = lax.broadcasted_iota(jnp.int32, (tm, LANES), 1)
    gvalid = (lane >= GROUP_LANE0) & (lane < GROUP_LANE0 + N_GROUPS)
    gmax, _ = _lane_argmax(logits, gvalid, lane)
    gexp = jnp.where(gvalid, jnp.exp(jnp.where(gvalid, logits - gmax, 0.0)), 0.0)
    p_grp = gexp / jnp.sum(gexp, axis=-1, keepdims=True)
    p_g, g_lane = _lane_argmax(p_grp, gvalid, lane)
    g_sel = g_lane - GROUP_LANE0
    evalid = (lane < N_EXPERTS) & ((lane // EXPERTS_PER_GROUP) == g_sel)
    emax, _ = _lane_argmax(logits, evalid, lane)
    eexp = jnp.where(evalid, jnp.exp(jnp.where(evalid, logits - emax, 0.0)), 0.0)
    p_e = eexp / jnp.sum(eexp, axis=-1, keepdims=True)
    p1, e1 = _lane_argmax(p_e, evalid, lane)
    p2, e2 = _lane_argmax(p_e, evalid & (lane != e1), lane)
    psum = p1 + p2
    gate1 = p_g * p1 / psum
    gate2 = p_g * p2 / psum

    hot1 = (lane == e1).astype(F32)
    hot2 = (lane == e2).astype(F32)
    hot = hot1 + hot2
    trow = lax.broadcasted_iota(jnp.int32, (tm, tm), 0)
    tcol = lax.broadcasted_iota(jnp.int32, (tm, tm), 1)
    before = (trow > tcol).astype(BF16)
    seen = base_ref[...] + _dot(before, hot.astype(BF16))
    rank1 = jnp.sum(hot1 * seen, axis=-1, keepdims=True)
    rank2 = jnp.sum(hot2 * seen, axis=-1, keepdims=True)
    base_ref[...] = base_ref[...] + jnp.sum(hot, axis=0, keepdims=True)
    cnt_ref[...] = base_ref[...]

    route = jnp.zeros((tm, LANES), F32)
    for pos, val in ((ROUTE_GATE0, gate1), (ROUTE_GATE1, gate2),
                     (ROUTE_EXPERT0, e1.astype(F32)), (ROUTE_EXPERT1, e2.astype(F32)),
                     (ROUTE_RANK0, rank1), (ROUTE_RANK1, rank2)):
        route = jnp.where(lane == pos, val, route)
    route_ref[...] = route
    plan_ref[...] = jnp.transpose(route)[:SUBLANES, :]


def _out_router(x, ydn, ycf, wo1, wo2, nw, wr, br, tm=512):
    T, D = x.shape
    full = lambda a: pl.BlockSpec(a.shape, lambda i: (0, 0), pipeline_mode=pl.Buffered(1))
    rows = lambda n: pl.BlockSpec((tm, n), lambda i: (i, 0))
    wr_hi = wr.astype(BF16)
    wr_lo = (wr - wr_hi.astype(F32)).astype(BF16)
    return pl.pallas_call(
        _out_router_kernel,
        grid=(T // tm,),
        in_specs=[rows(D), rows(DN_WIDTH), rows(CONF_WIDTH), full(wo1), full(wo2), full(nw),
                  full(wr_hi), full(wr_lo), full(br)],
        out_specs=[rows(D), rows(D), rows(LANES), pl.BlockSpec((SUBLANES, tm), lambda i: (0, i)),
                   pl.BlockSpec((1, LANES), lambda i: (0, 0))],
        out_shape=[jax.ShapeDtypeStruct((T, D), F32), jax.ShapeDtypeStruct((T, D), F32),
                   jax.ShapeDtypeStruct((T, LANES), F32), jax.ShapeDtypeStruct((SUBLANES, T), F32),
                   jax.ShapeDtypeStruct((1, LANES), F32)],
        scratch_shapes=[pltpu.VMEM((1, LANES), F32)],
        compiler_params=_params("arbitrary"),
        name="out_router",
    )(x, ydn, ycf, wo1, wo2, nw, wr_hi, wr_lo, br)


def _experts_kernel(tok_ref, be_ref, nused_ref, h_ref, wg_ref, wu_ref, wd_ref, ys_ref,
                    buf_a, buf_b, sem_a, sem_b, wg_s, wu_s, wd_s):
    i = pl.program_id(0)
    n_used = nused_ref[0]

    def gather(block, half, buf, sem):
        base = block * SLOT_BLOCK + half * GATHER_ROWS
        for r in range(GATHER_ROWS):
            pltpu.make_async_copy(h_ref.at[pl.ds(tok_ref[base + r], 1)], buf.at[pl.ds(r, 1)],
                                  sem).start()

    def gathered(buf, sem):
        pltpu.make_async_copy(h_ref.at[pl.ds(0, GATHER_ROWS)], buf, sem).wait()

    def ffn(buf, half):
        xb = buf[...].astype(BF16)
        act = _silu(_dot(xb, wg_s[...])) * _dot(xb, wu_s[...])
        ys_ref[half * GATHER_ROWS:(half + 1) * GATHER_ROWS, :] = _dot(act.astype(BF16), wd_s[...])

    @pl.when(i < n_used)
    def _():
        @pl.when(i == 0)
        def _():
            gather(0, 0, buf_a, sem_a)

        changed = jnp.logical_or(i == 0, be_ref[i] != be_ref[jnp.maximum(i - 1, 0)])

        @pl.when(changed)
        def _():
            wg_s[...] = wg_ref[0, 0].astype(BF16)
            wu_s[...] = wu_ref[0, 0].astype(BF16)
            wd_s[...] = wd_ref[0, 0].astype(BF16)

        gathered(buf_a, sem_a)
        gather(i, 1, buf_b, sem_b)
        ffn(buf_a, 0)
        gathered(buf_b, sem_b)
        gather(jnp.minimum(i + 1, n_used - 1), 0, buf_a, sem_a)
        ffn(buf_b, 1)

        @pl.when(i + 1 >= n_used)
        def _():
            gathered(buf_a, sem_a)

    @pl.when(i >= n_used)
    def _():
        ys_ref[...] = jnp.zeros_like(ys_ref)


def _experts(slot_tok, block_expert, n_used, h, w_gate, w_up, w_down, layer):
    T, D = h.shape
    P = slot_tok.shape[0]
    wsel = lambda i, tok, be, nu: (layer, be[i], 0, 0)
    return pl.pallas_call(
        _experts_kernel,
        grid_spec=pltpu.PrefetchScalarGridSpec(
            num_scalar_prefetch=3,
            grid=(P // SLOT_BLOCK,),
            in_specs=[pl.BlockSpec(memory_space=pl.ANY),
                      pl.BlockSpec((1, 1, D, D_EXPERT), wsel),
                      pl.BlockSpec((1, 1, D, D_EXPERT), wsel),
                      pl.BlockSpec((1, 1, D_EXPERT, D), wsel)],
            out_specs=pl.BlockSpec((SLOT_BLOCK, D), lambda i, tok, be, nu: (i, 0)),
            scratch_shapes=[pltpu.VMEM((GATHER_ROWS, D), F32), pltpu.VMEM((GATHER_ROWS, D), F32),
                            pltpu.SemaphoreType.DMA(()), pltpu.SemaphoreType.DMA(()),
                            pltpu.VMEM((D, D_EXPERT), BF16), pltpu.VMEM((D, D_EXPERT), BF16),
                            pltpu.VMEM((D_EXPERT, D), BF16)],
        ),
        out_shape=jax.ShapeDtypeStruct((P, D), F32),
        compiler_params=_params("arbitrary"),
        name="experts",
    )(slot_tok, block_expert, n_used, h, w_gate, w_up, w_down)


def _combine_kernel(dest_ref, x_ref, route_ref, nw_ref, ys_ref, o_ref, gbuf_ref, sem, *, final_norm):
    tm = x_ref.shape[0]
    base = pl.program_id(0) * tm

    def row_copy(r, k):
        return pltpu.make_async_copy(ys_ref.at[pl.ds(dest_ref[2 * (base + r) + k], 1)],
                                     gbuf_ref.at[k, pl.ds(r, 1)], sem)

    def start(r, carry):
        row_copy(r, 0).start()
        row_copy(r, 1).start()
        return carry

    lax.fori_loop(0, tm, start, 0)
    for k in range(2):
        pltpu.make_async_copy(ys_ref.at[pl.ds(0, tm)], gbuf_ref.at[k], sem).wait()
    route = route_ref[...]
    g0 = route[:, ROUTE_GATE0:ROUTE_GATE0 + 1]
    g1 = route[:, ROUTE_GATE1:ROUTE_GATE1 + 1]
    x = x_ref[...] + (gbuf_ref[0] * g0 + gbuf_ref[1] * g1)
    o_ref[...] = _rmsnorm(x, nw_ref[...]) if final_norm else x


def _combine(dest, x, route, nw, ys, final_norm, tm=256):
    T, D = x.shape
    return pl.pallas_call(
        functools.partial(_combine_kernel, final_norm=final_norm),
        grid_spec=pltpu.PrefetchScalarGridSpec(
            num_scalar_prefetch=1,
            grid=(T // tm,),
            in_specs=[pl.BlockSpec((tm, D), lambda i, d: (i, 0)),
                      pl.BlockSpec((tm, LANES), lambda i, d: (i, 0)),
                      pl.BlockSpec((1, D), lambda i, d: (0, 0)),
                      pl.BlockSpec(memory_space=pl.ANY)],
            out_specs=pl.BlockSpec((tm, D), lambda i, d: (i, 0)),
            scratch_shapes=[pltpu.VMEM((2, tm, D), F32), pltpu.SemaphoreType.DMA(())],
        ),
        out_shape=jax.ShapeDtypeStruct((T, D), F32),
        compiler_params=_params("arbitrary"),
        name="combine",
    )(dest, x, route, nw, ys)


def _slot_plan(plan, counts_row, n_tokens):
    n_blocks = 2 * n_tokens // SLOT_BLOCK + N_EXPERTS
    counts = counts_row[0, :N_EXPERTS].astype(jnp.int32)
    padded = (counts + SLOT_BLOCK - 1) // SLOT_BLOCK * SLOT_BLOCK
    pad_ends = jnp.cumsum(padded)
    pad_starts = pad_ends - padded
    expert = plan[ROUTE_EXPERT0:ROUTE_EXPERT1 + 1].astype(jnp.int32)
    rank = plan[ROUTE_RANK0:ROUTE_RANK1 + 1].astype(jnp.int32)
    dest = pad_starts[expert] + rank
    tokens = jnp.broadcast_to(jnp.arange(n_tokens, dtype=jnp.int32), dest.shape)
    slot_tok = jnp.zeros((n_blocks * SLOT_BLOCK,), jnp.int32).at[dest.reshape(-1)].set(
        tokens.reshape(-1), unique_indices=True)
    n_used = pad_ends[-1] // SLOT_BLOCK
    first_row = jnp.minimum(jnp.arange(n_blocks, dtype=jnp.int32), n_used - 1) * SLOT_BLOCK
    block_expert = jnp.sum(pad_ends[None, :] <= first_row[:, None], axis=1).astype(jnp.int32)
    block_expert = jnp.minimum(block_expert, N_EXPERTS - 1)
    return dest.T.reshape(-1), slot_tok, block_expert, n_used.astype(jnp.int32).reshape(1)


def _lane_pad(a, lane0, width=LANES):
    return jnp.pad(a.astype(F32), ((0, 0), (lane0, width - lane0 - a.shape[1])))


def kernel(x, norm_mix, w_in, conv_qkv, a_log, dt_bias, o_norm, w_dw, b_dw, ln_g, ln_b,
           w_out, norm_ffn, w_rg, b_rg, w_re, b_re, w_gate, w_up, w_down, norm_final):
    batch, seq, d_model = x.shape
    T = batch * seq
    depth = w_in.shape[0]
    o_z = 3 * DN_WIDTH
    o_b = 4 * DN_WIDTH
    o_c = o_b + 2 * DN_HEADS
    xt = x.reshape(T, d_model)
    for l in range(depth):
        wi = w_in[l]
        qkv, z, ba, glu = _in_proj(
            xt, norm_mix[l][None, :], wi[:, :o_z].astype(BF16), wi[:, o_z:o_b].astype(BF16),
            _lane_pad(wi[:, o_b:o_c], 0).astype(BF16), wi[:, o_c:].astype(BF16))
        y_dn = _deltanet(qkv, z, ba, conv_qkv[l], _lane_pad(a_log[l][None, :], DECAY_LANE0),
                         _lane_pad(dt_bias[l][None, :], DECAY_LANE0), o_norm[l][None, :], batch)
        y_cf = _conformer(glu, w_dw[l], b_dw[l][None, :], ln_g[l][None, :], ln_b[l][None, :], batch)
        wr = jnp.concatenate([w_re[l], _lane_pad(w_rg[l], 0, LANES - N_EXPERTS)], axis=1)
        br = jnp.concatenate([b_re[l][None, :], _lane_pad(b_rg[l][None, :], 0, LANES - N_EXPERTS)], axis=1)
        x2, h, route, plan, counts = _out_router(
            xt, y_dn, y_cf, w_out[l][:DN_WIDTH].astype(BF16), w_out[l][DN_WIDTH:].astype(BF16),
            norm_ffn[l][None, :], wr, br)
        dest, slot_tok, block_expert, n_used = _slot_plan(plan, counts, T)
        ys = _experts(slot_tok, block_expert, n_used, h, w_gate, w_up, w_down, l)
        last = l == depth - 1
        xt = _combine(dest, x2, route, norm_final[None, :] if last else norm_ffn[l][None, :],
                      ys, final_norm=last)
    return xt.reshape(batch, seq, d_model)
```

```python
import functools

import jax
import jax.numpy as jnp
from jax import lax
from jax.experimental import pallas as pl
from jax.experimental.pallas import tpu as pltpu

EPS = 1e-6
CHUNK = 64
DN_HEADS = 4
DN_HEAD_DIM = 128
DN_WIDTH = DN_HEADS * DN_HEAD_DIM
SHORT_CONV = 4
CONF_WIDTH = 512
CONF_KERNEL = 31
N_GROUPS = 4
EXPERTS_PER_GROUP = 8
N_EXPERTS = N_GROUPS * EXPERTS_PER_GROUP
D_EXPERT = 512
LANES = 128
SUBLANES = 8
GROUP_LANE0 = N_EXPERTS
BETA_LANE0 = 0
DECAY_LANE0 = DN_HEADS
SLOT_BLOCK = 512
GATHER_ROWS = SLOT_BLOCK // 2
TOKEN_TILE = 8
CHUNKS_PER_ITER = 2
VMEM_LIMIT = 56 * 1024 * 1024

F32 = jnp.float32
BF16 = jnp.bfloat16
HIGHEST = lax.Precision.HIGHEST
NT_DIMS = (((1,), (1,)), ((), ()))
TN_DIMS = (((0,), (0,)), ((), ()))


def _params(*semantics):
    return pltpu.CompilerParams(dimension_semantics=semantics, vmem_limit_bytes=VMEM_LIMIT)


def _sigmoid(x):
    return 1.0 / (1.0 + jnp.exp(-x))


def _silu(x):
    return x * _sigmoid(x)


def _rmsnorm(x, w):
    return x * lax.rsqrt(jnp.mean(x * x, axis=-1, keepdims=True) + EPS) * w


def _dot(a, b, **kw):
    return jnp.dot(a, b, preferred_element_type=F32, **kw)


def _store_token_tiles(ref, row0, x):
    n = x.shape[0]
    for j in range(TOKEN_TILE):
        ref[pl.ds(row0 * TOKEN_TILE + j, n, stride=TOKEN_TILE), :] = x[:, j * LANES:(j + 1) * LANES]


def _load_token_tiles(ref, n):
    return jnp.concatenate([ref[pl.ds(j, n, stride=TOKEN_TILE), :] for j in range(TOKEN_TILE)], axis=1)


def _tile_rows(row):
    return pl.ds(pl.multiple_of(row * TOKEN_TILE, TOKEN_TILE), TOKEN_TILE)


def _in_proj_kernel(x_ref, nw_ref, wqkv_ref, wz_ref, wba_ref, wglu_ref,
                    qkv_ref, z_ref, ba_ref, glu_ref):
    hb = _rmsnorm(x_ref[...], nw_ref[...]).astype(BF16)
    qkv_ref[...] = _dot(hb, wqkv_ref[...])
    z_ref[...] = _dot(hb, wz_ref[...])
    ba_ref[...] = _dot(hb, wba_ref[...])
    glu_ref[...] = _dot(hb, wglu_ref[...])


def _in_proj(x, nw, wqkv, wz, wba, wglu, tm=512):
    T, D = x.shape
    full = lambda a: pl.BlockSpec(a.shape, lambda i: (0, 0), pipeline_mode=pl.Buffered(1))
    rows = lambda n: pl.BlockSpec((tm, n), lambda i: (i, 0))
    widths = (wqkv.shape[1], wz.shape[1], wba.shape[1], wglu.shape[1])
    return pl.pallas_call(
        _in_proj_kernel,
        grid=(T // tm,),
        in_specs=[rows(D), full(nw), full(wqkv), full(wz), full(wba), full(wglu)],
        out_specs=[rows(n) for n in widths],
        out_shape=[jax.ShapeDtypeStruct((T, n), F32) for n in widths],
        compiler_params=_params("parallel"),
        name="in_proj",
    )(x, nw, wqkv, wz, wba, wglu)


def _deltanet_kernel(qkv_ref, z_ref, ba_ref, cw_ref, alog_ref, dtb_ref, onorm_ref, y_ref,
                     state_ref, tail_ref, cbuf_ref, q_s, k_s, v_s, beta_s, gc_s, gct_s, gl_s,
                     u_s, ws_s, vn_s):
    tb = qkv_ref.shape[0]
    n_chunks = tb // CHUNK
    halo = SUBLANES

    @pl.when(pl.program_id(1) == 0)
    def _():
        state_ref[...] = jnp.zeros_like(state_ref)
        tail_ref[...] = jnp.zeros_like(tail_ref)

    cbuf_ref[0:halo, :] = tail_ref[...]
    cbuf_ref[halo:halo + tb, :] = qkv_ref[...]
    tail_ref[...] = qkv_ref[tb - halo:tb, :]
    first = halo - (SHORT_CONV - 1)
    acc = cw_ref[0:1, :] * cbuf_ref[first:first + tb, :]
    for j in range(1, SHORT_CONV):
        acc = acc + cw_ref[j:j + 1, :] * cbuf_ref[first + j:first + j + tb, :]
    c = _silu(acc)

    for h in range(DN_HEADS):
        hs = slice(h * DN_HEAD_DIM, (h + 1) * DN_HEAD_DIM)
        q = c[:, h * DN_HEAD_DIM:(h + 1) * DN_HEAD_DIM]
        k = c[:, DN_WIDTH + h * DN_HEAD_DIM:DN_WIDTH + (h + 1) * DN_HEAD_DIM]
        q_s[:, hs] = q * lax.rsqrt(jnp.sum(q * q, axis=-1, keepdims=True) + EPS) * (DN_HEAD_DIM ** -0.5)
        k_s[:, hs] = k * lax.rsqrt(jnp.sum(k * k, axis=-1, keepdims=True) + EPS)
    v_s[...] = c[:, 2 * DN_WIDTH:3 * DN_WIDTH]

    ba = ba_ref[...]
    beta_s[...] = _sigmoid(ba)
    xa = ba + dtb_ref[...]
    softplus = jnp.maximum(xa, 0.0) + jnp.log1p(jnp.exp(-jnp.abs(xa)))
    g = -jnp.exp(alog_ref[...]) * softplus

    pair = 2 * DN_HEAD_DIM
    row = lax.broadcasted_iota(jnp.int32, (CHUNK, LANES), 0)
    lane = lax.broadcasted_iota(jnp.int32, (CHUNK, LANES), 1)
    second = lane >= CHUNK
    col = jnp.where(second, lane - CHUNK, lane)
    incl = row >= col
    strict = row > col
    eye = (row == col).astype(F32)
    trow = lax.broadcasted_iota(jnp.int32, (CHUNK, CHUNK), 0)
    tcol = lax.broadcasted_iota(jnp.int32, (CHUNK, CHUNK), 1)
    tril = (trow >= tcol).astype(F32)
    sel = (lax.broadcasted_iota(jnp.int32, (SUBLANES, LANES), 0)
           == lax.broadcasted_iota(jnp.int32, (SUBLANES, LANES), 1)).astype(F32)

    for ci in range(n_chunks):
        gc = _dot(tril, g[ci * CHUNK:(ci + 1) * CHUNK, :], precision=HIGHEST)
        gc_s[ci * CHUNK:(ci + 1) * CHUNK, :] = gc
        nxt = pltpu.roll(gc, LANES - 1, 1)
        gct_s[ci] = lax.dot_general(sel, jnp.concatenate([gc, nxt], axis=0), NT_DIMS,
                                    precision=HIGHEST, preferred_element_type=F32)
        gl_s[ci:ci + 1, :] = jnp.exp(gc[CHUNK - 1:CHUNK, :])

    def block_diag(m):
        half = m.shape[1] // 2
        z = jnp.zeros_like(m[:, :half])
        return jnp.concatenate([jnp.concatenate([m[:, :half], z], axis=1),
                                jnp.concatenate([z, m[:, half:]], axis=1)], axis=0)

    def block_diag_lanes(m):
        return jnp.concatenate([jnp.where(second, 0.0, m), jnp.where(second, m, 0.0)], axis=0)

    def intra_chunk(it, carry):
        streams = [(it * CHUNKS_PER_ITER + sub, p) for sub in range(CHUNKS_PER_ITER)
                   for p in range(DN_HEADS // 2)]
        st = []
        for ci, p in streams:
            rows = pl.ds(pl.multiple_of(ci * CHUNK, CHUNK), CHUNK)
            ps = slice(p * pair, (p + 1) * pair)
            gc = gc_s[rows, :]
            l0 = DECAY_LANE0 + 2 * p
            gcol0, gcol1 = gc[:, l0:l0 + 1], gc[:, l0 + 1:l0 + 2]
            gcol = jnp.where(second, gcol1, gcol0)
            grow = gct_s[ci][l0:l0 + 1, :]
            decay = jnp.where(incl, jnp.exp(jnp.where(incl, gcol - grow, 0.0)), 0.0)
            q, k, v = q_s[rows, ps], k_s[rows, ps], v_s[rows, ps]
            b0 = beta_s[rows, BETA_LANE0 + 2 * p:BETA_LANE0 + 2 * p + 1]
            b1 = beta_s[rows, BETA_LANE0 + 2 * p + 1:BETA_LANE0 + 2 * p + 2]
            kkqk = lax.dot_general(jnp.concatenate([k, q], axis=0).astype(BF16),
                                   block_diag(k.astype(BF16)), NT_DIMS, preferred_element_type=F32)
            st.append(dict(ci=ci, p=p, rows=rows, ps=ps, gc=gc, l0=l0, gcol0=gcol0, gcol1=gcol1,
                           decay=decay, q=q, k=k, v=v, b0=b0, b1=b1, kkqk=kkqk))
        for s in st:
            a = jnp.where(strict, jnp.where(second, s['b1'], s['b0']) * s['kkqk'][:CHUNK] * s['decay'], 0.0)
            s['inv'] = eye - a
            s['pw'] = _dot(a.astype(BF16), block_diag_lanes(a).astype(BF16))
        for _ in range(4):
            for s in st:
                r = _dot(jnp.concatenate([s['inv'], s['pw']], axis=0).astype(BF16),
                         block_diag_lanes(s['pw']).astype(BF16))
                s['inv'] = s['inv'] + r[:CHUNK]
                s['pw'] = r[CHUNK:]
        for s in st:
            s['inv'] = s['inv'] + _dot(s['inv'].astype(BF16), block_diag_lanes(s['pw']).astype(BF16))
        for s in st:
            eg0, eg1 = jnp.exp(s['gcol0']), jnp.exp(s['gcol1'])
            k, v, q = s['k'], s['v'], s['q']
            hd = DN_HEAD_DIM
            rhs = jnp.concatenate([s['b0'] * v[:, :hd], (s['b0'] * eg0) * k[:, :hd],
                                   s['b1'] * v[:, hd:], (s['b1'] * eg1) * k[:, hd:]], axis=1)
            sol = _dot(s['inv'].astype(BF16), block_diag(rhs.astype(BF16)))
            ci, p = s['ci'], s['p']
            u_s[s['rows'], s['ps']] = jnp.concatenate([sol[:, :hd], sol[:, 2 * hd:3 * hd]], axis=1)
            ws_s[ci, p, 0:CHUNK, :] = jnp.concatenate(
                [sol[:, hd:2 * hd], sol[:, 3 * hd:]], axis=1).astype(BF16)
            ws_s[ci, p, CHUNK:2 * CHUNK, :] = jnp.concatenate(
                [q[:, :hd] * eg0, q[:, hd:] * eg1], axis=1).astype(BF16)
            vn_s[ci, p, 0:CHUNK, :] = (s['kkqk'][CHUNK:] * s['decay']).astype(BF16)
            gc, l0 = s['gc'], s['l0']
            glast0 = gc[CHUNK - 1:CHUNK, l0:l0 + 1]
            glast1 = gc[CHUNK - 1:CHUNK, l0 + 1:l0 + 2]
            kdec = jnp.concatenate([k[:, :hd] * jnp.exp(glast0 - s['gcol0']),
                                    k[:, hd:] * jnp.exp(glast1 - s['gcol1'])], axis=0)
            vn_s[ci, p, CHUNK:CHUNK + hd, :] = jnp.transpose(kdec).astype(BF16)
        return carry

    lax.fori_loop(0, n_chunks // CHUNKS_PER_ITER, intra_chunk, 0)

    def inter_chunk(ci, carry):
        rows = pl.ds(pl.multiple_of(ci * CHUNK, CHUNK), CHUNK)
        glrow = gl_s[pl.ds(ci, 1), :]
        hd = DN_HEAD_DIM
        pairs = range(DN_HEADS // 2)
        r1 = [_dot(ws_s[ci, p], block_diag(jnp.concatenate(
            [state_ref[2 * p], state_ref[2 * p + 1]], axis=1).astype(BF16))) for p in pairs]
        v_new = [u_s[rows, p * pair:(p + 1) * pair] - r1[p][:CHUNK] for p in pairs]
        r2 = [_dot(vn_s[ci, p], block_diag(v_new[p].astype(BF16))) for p in pairs]
        for p in pairs:
            o2 = r1[p][CHUNK:] + r2[p][:CHUNK]
            for j in range(2):
                h = 2 * p + j
                hs = slice(h * hd, (h + 1) * hd)
                lane_h = DECAY_LANE0 + h
                state_ref[h] = (state_ref[h] * glrow[:, lane_h:lane_h + 1]
                                + r2[p][CHUNK:, j * hd:(j + 1) * hd])
                o = o2[:, j * hd:(j + 1) * hd]
                o = o * lax.rsqrt(jnp.mean(o * o, axis=-1, keepdims=True) + EPS) * onorm_ref[...]
                y_ref[rows, hs] = (o * _silu(z_ref[rows, hs])).astype(y_ref.dtype)
        return carry

    lax.fori_loop(0, n_chunks, inter_chunk, 0)


def _deltanet(qkv, z, ba, cw, alog, dtb, onorm, batch, tb=512):
    T = qkv.shape[0]
    nb = T // batch // tb
    n_chunks = tb // CHUNK
    full = lambda a: pl.BlockSpec(a.shape, lambda b, i: (0, 0))
    rows = lambda n: pl.BlockSpec((tb, n), lambda b, i: (b * nb + i, 0))
    scr = lambda *s: pltpu.VMEM(s, F32)
    return pl.pallas_call(
        _deltanet_kernel,
        grid=(batch, nb),
        in_specs=[rows(3 * DN_WIDTH), rows(DN_WIDTH), rows(LANES),
                  full(cw), full(alog), full(dtb), full(onorm)],
        out_specs=rows(DN_WIDTH),
        out_shape=jax.ShapeDtypeStruct((T, DN_WIDTH), BF16),
        scratch_shapes=[
            scr(DN_HEADS, DN_HEAD_DIM, DN_HEAD_DIM),
            scr(SUBLANES, 3 * DN_WIDTH),
            scr(SUBLANES + tb, 3 * DN_WIDTH),
            scr(tb, DN_WIDTH), scr(tb, DN_WIDTH), scr(tb, DN_WIDTH),
            scr(tb, LANES),
            scr(tb, LANES), scr(n_chunks, SUBLANES, 2 * CHUNK), scr(n_chunks, LANES),
            scr(tb, DN_WIDTH),
            pltpu.VMEM((n_chunks, DN_HEADS // 2, 2 * CHUNK, 2 * DN_HEAD_DIM), BF16),
            pltpu.VMEM((n_chunks, DN_HEADS // 2, CHUNK + DN_HEAD_DIM, 2 * CHUNK), BF16),
        ],
        compiler_params=_params("parallel", "arbitrary"),
        name="deltanet",
    )(qkv, z, ba, cw, alog, dtb, onorm)


def _conformer_kernel(glu_ref, wdw_ref, bdw_ref, lng_ref, lnb_ref, y_ref, cbuf_ref):
    tc = glu_ref.shape[0]
    halo = 4 * SUBLANES

    @pl.when(pl.program_id(1) == 0)
    def _():
        cbuf_ref[0:halo, :] = jnp.zeros((halo, CONF_WIDTH), F32)

    u = glu_ref[...]
    g = u[:, :CONF_WIDTH] * _sigmoid(u[:, CONF_WIDTH:])
    cbuf_ref[halo:halo + tc, :] = g
    first = halo - (CONF_KERNEL - 1)
    acc = bdw_ref[...]
    for phase in range(SUBLANES):
        taps = [j for j in range(CONF_KERNEL) if (first + j) % SUBLANES == phase]
        n = tc if phase == 0 else tc + SUBLANES
        part = None
        for j in taps:
            r0 = first + j - phase
            term = wdw_ref[j:j + 1, :] * cbuf_ref[r0:r0 + n, :]
            part = term if part is None else part + term
        acc = acc + part[phase:phase + tc, :]
    cbuf_ref[0:halo, :] = cbuf_ref[tc:tc + halo, :]
    mu = jnp.mean(acc, axis=-1, keepdims=True)
    d = acc - mu
    var = jnp.mean(d * d, axis=-1, keepdims=True)
    y = d * lax.rsqrt(var + EPS) * lng_ref[...] + lnb_ref[...]
    y_ref[...] = _silu(y).astype(y_ref.dtype)


def _conformer(glu, wdw, bdw, lng, lnb, batch, tc=512):
    T = glu.shape[0]
    nb = T // batch // tc
    full = lambda a: pl.BlockSpec(a.shape, lambda b, i: (0, 0))
    rows = lambda n: pl.BlockSpec((tc, n), lambda b, i: (b * nb + i, 0))
    return pl.pallas_call(
        _conformer_kernel,
        grid=(batch, nb),
        in_specs=[rows(2 * CONF_WIDTH), full(wdw), full(bdw), full(lng), full(lnb)],
        out_specs=rows(CONF_WIDTH),
        out_shape=jax.ShapeDtypeStruct((T, CONF_WIDTH), BF16),
        scratch_shapes=[pltpu.VMEM((4 * SUBLANES + tc, CONF_WIDTH), F32)],
        compiler_params=_params("parallel", "arbitrary"),
        name="conformer",
    )(glu, wdw, bdw, lng, lnb)


ROUTE_GATE0, ROUTE_GATE1, ROUTE_EXPERT0, ROUTE_EXPERT1, ROUTE_RANK0, ROUTE_RANK1 = range(6)


def _lane_argmax(vals, valid, lane):
    neg = jnp.float32(-1e30)
    m = jnp.max(jnp.where(valid, vals, neg), axis=-1, keepdims=True)
    idx = jnp.min(jnp.where(valid & (vals == m), lane, LANES), axis=-1, keepdims=True)
    return m, idx


def _out_router_kernel(x_ref, ydn_ref, ycf_ref, wo1_ref, wo2_ref, nw_ref, wrh_ref, wrl_ref, br_ref,
                       x2_ref, h_ref, route_ref, plan_ref, cnt_ref, base_ref):
    tm = x_ref.shape[0]

    @pl.when(pl.program_id(0) == 0)
    def _():
        base_ref[...] = jnp.zeros_like(base_ref)

    x2 = x_ref[...] + _dot(ydn_ref[...], wo1_ref[...]) + _dot(ycf_ref[...], wo2_ref[...])
    x2_ref[...] = x2
    h = _rmsnorm(x2, nw_ref[...])
    _store_token_tiles(h_ref, 0, h)
    h_hi = h.astype(BF16)
    h_lo = (h - h_hi.astype(F32)).astype(BF16)
    hh = _dot(jnp.concatenate([h_hi, h_lo], axis=0), wrh_ref[...])
    logits = hh[:tm] + hh[tm:] + _dot(h_hi, wrl_ref[...]) + br_ref[...]

    lane = lax.broadcasted_iota(jnp.int32, (tm, LANES), 1)
    gvalid = (lane >= GROUP_LANE0) & (lane < GROUP_LANE0 + N_GROUPS)
    gmax, _ = _lane_argmax(logits, gvalid, lane)
    gexp = jnp.where(gvalid, jnp.exp(jnp.where(gvalid, logits - gmax, 0.0)), 0.0)
    p_grp = gexp / jnp.sum(gexp, axis=-1, keepdims=True)
    p_g, g_lane = _lane_argmax(p_grp, gvalid, lane)
    g_sel = g_lane - GROUP_LANE0
    evalid = (lane < N_EXPERTS) & ((lane // EXPERTS_PER_GROUP) == g_sel)
    emax, _ = _lane_argmax(logits, evalid, lane)
    eexp = jnp.where(evalid, jnp.exp(jnp.where(evalid, logits - emax, 0.0)), 0.0)
    p_e = eexp / jnp.sum(eexp, axis=-1, keepdims=True)
    p1, e1 = _lane_argmax(p_e, evalid, lane)
    p2, e2 = _lane_argmax(p_e, evalid & (lane != e1), lane)
    psum = p1 + p2
    gate1 = p_g * p1 / psum
    gate2 = p_g * p2 / psum

    hot1 = (lane == e1).astype(F32)
    hot2 = (lane == e2).astype(F32)
    hot = hot1 + hot2
    trow = lax.broadcasted_iota(jnp.int32, (tm, tm), 0)
    tcol = lax.broadcasted_iota(jnp.int32, (tm, tm), 1)
    before = (trow > tcol).astype(BF16)
    seen = base_ref[...] + _dot(before, hot.astype(BF16))
    rank1 = jnp.sum(hot1 * seen, axis=-1, keepdims=True)
    rank2 = jnp.sum(hot2 * seen, axis=-1, keepdims=True)
    base_ref[...] = base_ref[...] + jnp.sum(hot, axis=0, keepdims=True)
    cnt_ref[...] = base_ref[...]

    route = jnp.zeros((tm, LANES), F32)
    for pos, val in ((ROUTE_GATE0, gate1), (ROUTE_GATE1, gate2),
                     (ROUTE_EXPERT0, e1.astype(F32)), (ROUTE_EXPERT1, e2.astype(F32)),
                     (ROUTE_RANK0, rank1), (ROUTE_RANK1, rank2)):
        route = jnp.where(lane == pos, val, route)
    route_ref[...] = route
    plan_ref[...] = jnp.transpose(route)[:SUBLANES, :]


def _out_router(x, ydn, ycf, wo1, wo2, nw, wr, br, tm=512):
    T, D = x.shape
    full = lambda a: pl.BlockSpec(a.shape, lambda i: (0, 0), pipeline_mode=pl.Buffered(1))
    rows = lambda n: pl.BlockSpec((tm, n), lambda i: (i, 0))
    wr_hi = wr.astype(BF16)
    wr_lo = (wr - wr_hi.astype(F32)).astype(BF16)
    return pl.pallas_call(
        _out_router_kernel,
        grid=(T // tm,),
        in_specs=[rows(D), rows(DN_WIDTH), rows(CONF_WIDTH), full(wo1), full(wo2), full(nw),
                  full(wr_hi), full(wr_lo), full(br)],
        out_specs=[rows(D), pl.BlockSpec((tm * TOKEN_TILE, LANES), lambda i: (i, 0)), rows(LANES),
                   pl.BlockSpec((SUBLANES, tm), lambda i: (0, i)),
                   pl.BlockSpec((1, LANES), lambda i: (0, 0))],
        out_shape=[jax.ShapeDtypeStruct((T, D), F32),
                   jax.ShapeDtypeStruct((T * TOKEN_TILE, LANES), F32),
                   jax.ShapeDtypeStruct((T, LANES), F32), jax.ShapeDtypeStruct((SUBLANES, T), F32),
                   jax.ShapeDtypeStruct((1, LANES), F32)],
        scratch_shapes=[pltpu.VMEM((1, LANES), F32)],
        compiler_params=_params("arbitrary"),
        name="out_router",
    )(x, ydn, ycf, wo1, wo2, nw, wr_hi, wr_lo, br)


def _experts_kernel(tok_ref, be_ref, nused_ref, h_ref, wg_ref, wu_ref, wd_ref, ys_ref,
                    buf_a, buf_b, sem_a, sem_b, wg_s, wu_s, wd_s):
    i = pl.program_id(0)
    n_used = nused_ref[0]

    def gather(block, half, buf, sem):
        base = block * SLOT_BLOCK + half * GATHER_ROWS
        for r in range(GATHER_ROWS):
            pltpu.make_async_copy(h_ref.at[_tile_rows(tok_ref[base + r])],
                                  buf.at[pl.ds(r * TOKEN_TILE, TOKEN_TILE)], sem).start()

    def gathered(buf, sem):
        pltpu.make_async_copy(h_ref.at[pl.ds(0, GATHER_ROWS * TOKEN_TILE)], buf, sem).wait()

    def ffn(buf, half):
        xb = _load_token_tiles(buf, GATHER_ROWS).astype(BF16)
        act = _silu(_dot(xb, wg_s[...])) * _dot(xb, wu_s[...])
        _store_token_tiles(ys_ref, half * GATHER_ROWS, _dot(act.astype(BF16), wd_s[...]))

    @pl.when(i < n_used)
    def _():
        @pl.when(i == 0)
        def _():
            gather(0, 0, buf_a, sem_a)

        changed = jnp.logical_or(i == 0, be_ref[i] != be_ref[jnp.maximum(i - 1, 0)])

        @pl.when(changed)
        def _():
            wg_s[...] = wg_ref[0, 0].astype(BF16)
            wu_s[...] = wu_ref[0, 0].astype(BF16)
            wd_s[...] = wd_ref[0, 0].astype(BF16)

        gathered(buf_a, sem_a)
        gather(i, 1, buf_b, sem_b)
        ffn(buf_a, 0)
        gathered(buf_b, sem_b)
        gather(jnp.minimum(i + 1, n_used - 1), 0, buf_a, sem_a)
        ffn(buf_b, 1)

        @pl.when(i + 1 >= n_used)
        def _():
            gathered(buf_a, sem_a)

    @pl.when(i >= n_used)
    def _():
        ys_ref[...] = jnp.zeros_like(ys_ref)


def _experts(slot_tok, block_expert, n_used, h, w_gate, w_up, w_down, layer):
    D = h.shape[1] * TOKEN_TILE
    P = slot_tok.shape[0]
    wsel = lambda i, tok, be, nu: (layer, be[i], 0, 0)
    return pl.pallas_call(
        _experts_kernel,
        grid_spec=pltpu.PrefetchScalarGridSpec(
            num_scalar_prefetch=3,
            grid=(P // SLOT_BLOCK,),
            in_specs=[pl.BlockSpec(memory_space=pl.ANY),
                      pl.BlockSpec((1, 1, D, D_EXPERT), wsel),
                      pl.BlockSpec((1, 1, D, D_EXPERT), wsel),
                      pl.BlockSpec((1, 1, D_EXPERT, D), wsel)],
            out_specs=pl.BlockSpec((SLOT_BLOCK * TOKEN_TILE, LANES), lambda i, tok, be, nu: (i, 0)),
            scratch_shapes=[pltpu.VMEM((GATHER_ROWS * TOKEN_TILE, LANES), F32),
                            pltpu.VMEM((GATHER_ROWS * TOKEN_TILE, LANES), F32),
                            pltpu.SemaphoreType.DMA(()), pltpu.SemaphoreType.DMA(()),
                            pltpu.VMEM((D, D_EXPERT), BF16), pltpu.VMEM((D, D_EXPERT), BF16),
                            pltpu.VMEM((D_EXPERT, D), BF16)],
        ),
        out_shape=jax.ShapeDtypeStruct((P * TOKEN_TILE, LANES), F32),
        compiler_params=_params("arbitrary"),
        name="experts",
    )(slot_tok, block_expert, n_used, h, w_gate, w_up, w_down)


def _combine_kernel(dest_ref, x_ref, route_ref, nw_ref, ys_ref, o_ref, gbuf_ref, sem, *, final_norm):
    tm = x_ref.shape[0]
    base = pl.program_id(0) * tm

    def row_copy(r, k):
        return pltpu.make_async_copy(ys_ref.at[_tile_rows(dest_ref[2 * (base + r) + k])],
                                     gbuf_ref.at[k, _tile_rows(r)], sem)

    def start(r, carry):
        row_copy(r, 0).start()
        row_copy(r, 1).start()
        return carry

    lax.fori_loop(0, tm, start, 0)
    for k in range(2):
        pltpu.make_async_copy(ys_ref.at[pl.ds(0, tm * TOKEN_TILE)], gbuf_ref.at[k], sem).wait()
    route = route_ref[...]
    g0 = route[:, ROUTE_GATE0:ROUTE_GATE0 + 1]
    g1 = route[:, ROUTE_GATE1:ROUTE_GATE1 + 1]
    x = x_ref[...] + (_load_token_tiles(gbuf_ref.at[0], tm) * g0 + _load_token_tiles(gbuf_ref.at[1], tm) * g1)
    o_ref[...] = _rmsnorm(x, nw_ref[...]) if final_norm else x


def _combine(dest, x, route, nw, ys, final_norm, tm=256):
    T, D = x.shape
    return pl.pallas_call(
        functools.partial(_combine_kernel, final_norm=final_norm),
        grid_spec=pltpu.PrefetchScalarGridSpec(
            num_scalar_prefetch=1,
            grid=(T // tm,),
            in_specs=[pl.BlockSpec((tm, D), lambda i, d: (i, 0)),
                      pl.BlockSpec((tm, LANES), lambda i, d: (i, 0)),
                      pl.BlockSpec((1, D), lambda i, d: (0, 0)),
                      pl.BlockSpec(memory_space=pl.ANY)],
            out_specs=pl.BlockSpec((tm, D), lambda i, d: (i, 0)),
            scratch_shapes=[pltpu.VMEM((2, tm * TOKEN_TILE, LANES), F32), pltpu.SemaphoreType.DMA(())],
        ),
        out_shape=jax.ShapeDtypeStruct((T, D), F32),
        compiler_params=_params("arbitrary"),
        name="combine",
    )(dest, x, route, nw, ys)


def _slot_plan(plan, counts_row, n_tokens):
    n_blocks = 2 * n_tokens // SLOT_BLOCK + N_EXPERTS
    counts = counts_row[0, :N_EXPERTS].astype(jnp.int32)
    padded = (counts + SLOT_BLOCK - 1) // SLOT_BLOCK * SLOT_BLOCK
    pad_ends = jnp.cumsum(padded)
    pad_starts = pad_ends - padded
    expert = plan[ROUTE_EXPERT0:ROUTE_EXPERT1 + 1].astype(jnp.int32)
    rank = plan[ROUTE_RANK0:ROUTE_RANK1 + 1].astype(jnp.int32)
    hot = expert[..., None] == jnp.arange(N_EXPERTS, dtype=jnp.int32)
    dest = jnp.sum(jnp.where(hot, pad_starts, 0), axis=-1) + rank
    tokens = jnp.broadcast_to(jnp.arange(n_tokens, dtype=jnp.int32), dest.shape)
    slot_tok = jnp.zeros((n_blocks * SLOT_BLOCK,), jnp.int32).at[dest.reshape(-1)].set(
        tokens.reshape(-1), unique_indices=True)
    n_used = pad_ends[-1] // SLOT_BLOCK
    first_row = jnp.minimum(jnp.arange(n_blocks, dtype=jnp.int32), n_used - 1) * SLOT_BLOCK
    block_expert = jnp.sum(pad_ends[None, :] <= first_row[:, None], axis=1).astype(jnp.int32)
    block_expert = jnp.minimum(block_expert, N_EXPERTS - 1)
    return dest.T.reshape(-1), slot_tok, block_expert, n_used.astype(jnp.int32).reshape(1)


def _lane_pad(a, lane0, width=LANES):
    return jnp.pad(a.astype(F32), ((0, 0), (lane0, width - lane0 - a.shape[1])))


def kernel(x, norm_mix, w_in, conv_qkv, a_log, dt_bias, o_norm, w_dw, b_dw, ln_g, ln_b,
           w_out, norm_ffn, w_rg, b_rg, w_re, b_re, w_gate, w_up, w_down, norm_final):
    batch, seq, d_model = x.shape
    T = batch * seq
    depth = w_in.shape[0]
    o_z = 3 * DN_WIDTH
    o_b = 4 * DN_WIDTH
    o_c = o_b + 2 * DN_HEADS
    xt = x.reshape(T, d_model)
    for l in range(depth):
        wi = w_in[l]
        qkv, z, ba, glu = _in_proj(
            xt, norm_mix[l][None, :], wi[:, :o_z].astype(BF16), wi[:, o_z:o_b].astype(BF16),
            _lane_pad(wi[:, o_b:o_c], 0).astype(BF16), wi[:, o_c:].astype(BF16))
        y_dn = _deltanet(qkv, z, ba, conv_qkv[l], _lane_pad(a_log[l][None, :], DECAY_LANE0),
                         _lane_pad(dt_bias[l][None, :], DECAY_LANE0), o_norm[l][None, :], batch)
        y_cf = _conformer(glu, w_dw[l], b_dw[l][None, :], ln_g[l][None, :], ln_b[l][None, :], batch)
        wr = jnp.concatenate([w_re[l], _lane_pad(w_rg[l], 0, LANES - N_EXPERTS)], axis=1)
        br = jnp.concatenate([b_re[l][None, :], _lane_pad(b_rg[l][None, :], 0, LANES - N_EXPERTS)], axis=1)
        x2, h, route, plan, counts = _out_router(
            xt, y_dn, y_cf, w_out[l][:DN_WIDTH].astype(BF16), w_out[l][DN_WIDTH:].astype(BF16),
            norm_ffn[l][None, :], wr, br)
        dest, slot_tok, block_expert, n_used = _slot_plan(plan, counts, T)
        ys = _experts(slot_tok, block_expert, n_used, h, w_gate, w_up, w_down, l)
        last = l == depth - 1
        xt = _combine(dest, x2, route, norm_final[None, :] if last else norm_ffn[l][None, :],
                      ys, final_norm=last)
    return xt.reshape(batch, seq, d_model)
```

```python
import functools

import jax
import jax.numpy as jnp
from jax import lax
from jax.experimental import pallas as pl
from jax.experimental.pallas import tpu as pltpu

EPS = 1e-6
CHUNK = 64
DN_HEADS = 4
DN_HEAD_DIM = 128
DN_WIDTH = DN_HEADS * DN_HEAD_DIM
SHORT_CONV = 4
CONF_WIDTH = 512
CONF_KERNEL = 31
N_GROUPS = 4
EXPERTS_PER_GROUP = 8
N_EXPERTS = N_GROUPS * EXPERTS_PER_GROUP
D_EXPERT = 512
LANES = 128
SUBLANES = 8
GROUP_LANE0 = N_EXPERTS
BETA_LANE0 = 0
DECAY_LANE0 = DN_HEADS
SLOT_BLOCK = 512
GATHER_ROWS = SLOT_BLOCK // 2
TOKEN_TILE = 8
CHUNKS_PER_ITER = 4
VMEM_LIMIT = 56 * 1024 * 1024

F32 = jnp.float32
BF16 = jnp.bfloat16
HIGHEST = lax.Precision.HIGHEST
NT_DIMS = (((1,), (1,)), ((), ()))
TN_DIMS = (((0,), (0,)), ((), ()))


def _params(*semantics):
    return pltpu.CompilerParams(dimension_semantics=semantics, vmem_limit_bytes=VMEM_LIMIT)


def _sigmoid(x):
    return 1.0 / (1.0 + jnp.exp(-x))


def _silu(x):
    return x * _sigmoid(x)


def _rmsnorm(x, w):
    return x * lax.rsqrt(jnp.mean(x * x, axis=-1, keepdims=True) + EPS) * w


def _dot(a, b, **kw):
    return jnp.dot(a, b, preferred_element_type=F32, **kw)


def _store_token_tiles(ref, row0, x):
    n = x.shape[0]
    for j in range(TOKEN_TILE):
        ref[pl.ds(row0 * TOKEN_TILE + j, n, stride=TOKEN_TILE), :] = x[:, j * LANES:(j + 1) * LANES]


def _load_token_tiles(ref, n):
    return jnp.concatenate([ref[pl.ds(j, n, stride=TOKEN_TILE), :] for j in range(TOKEN_TILE)], axis=1)


def _tile_rows(row):
    return pl.ds(pl.multiple_of(row * TOKEN_TILE, TOKEN_TILE), TOKEN_TILE)


def _in_proj_kernel(x_ref, nw_ref, wqkv_ref, wz_ref, wba_ref, wglu_ref,
                    qkv_ref, z_ref, ba_ref, glu_ref):
    hb = _rmsnorm(x_ref[...], nw_ref[...]).astype(BF16)
    qkv_ref[...] = _dot(hb, wqkv_ref[...])
    z_ref[...] = _dot(hb, wz_ref[...])
    ba_ref[...] = _dot(hb, wba_ref[...])
    glu_ref[...] = _dot(hb, wglu_ref[...])


def _in_proj(x, nw, wqkv, wz, wba, wglu, tm=512):
    T, D = x.shape
    full = lambda a: pl.BlockSpec(a.shape, lambda i: (0, 0), pipeline_mode=pl.Buffered(1))
    rows = lambda n: pl.BlockSpec((tm, n), lambda i: (i, 0))
    widths = (wqkv.shape[1], wz.shape[1], wba.shape[1], wglu.shape[1])
    return pl.pallas_call(
        _in_proj_kernel,
        grid=(T // tm,),
        in_specs=[rows(D), full(nw), full(wqkv), full(wz), full(wba), full(wglu)],
        out_specs=[rows(n) for n in widths],
        out_shape=[jax.ShapeDtypeStruct((T, n), F32) for n in widths],
        compiler_params=_params("parallel"),
        name="in_proj",
    )(x, nw, wqkv, wz, wba, wglu)


def _deltanet_kernel(qkv_ref, z_ref, ba_ref, cw_ref, alog_ref, dtb_ref, onorm_ref, y_ref,
                     state_ref, tail_ref, cbuf_ref, q_s, k_s, v_s, beta_s, gc_s, gct_s, gl_s,
                     u_s, ws_s, vn_s):
    tb = qkv_ref.shape[0]
    n_chunks = tb // CHUNK
    halo = SUBLANES

    @pl.when(pl.program_id(1) == 0)
    def _():
        state_ref[...] = jnp.zeros_like(state_ref)
        tail_ref[...] = jnp.zeros_like(tail_ref)

    cbuf_ref[0:halo, :] = tail_ref[...]
    cbuf_ref[halo:halo + tb, :] = qkv_ref[...]
    tail_ref[...] = qkv_ref[tb - halo:tb, :]
    first = halo - (SHORT_CONV - 1)
    acc = cw_ref[0:1, :] * cbuf_ref[first:first + tb, :]
    for j in range(1, SHORT_CONV):
        acc = acc + cw_ref[j:j + 1, :] * cbuf_ref[first + j:first + j + tb, :]
    c = _silu(acc)

    for h in range(DN_HEADS):
        hs = slice(h * DN_HEAD_DIM, (h + 1) * DN_HEAD_DIM)
        q = c[:, h * DN_HEAD_DIM:(h + 1) * DN_HEAD_DIM]
        k = c[:, DN_WIDTH + h * DN_HEAD_DIM:DN_WIDTH + (h + 1) * DN_HEAD_DIM]
        q_s[:, hs] = q * lax.rsqrt(jnp.sum(q * q, axis=-1, keepdims=True) + EPS) * (DN_HEAD_DIM ** -0.5)
        k_s[:, hs] = k * lax.rsqrt(jnp.sum(k * k, axis=-1, keepdims=True) + EPS)
    v_s[...] = c[:, 2 * DN_WIDTH:3 * DN_WIDTH]

    ba = ba_ref[...]
    beta_s[...] = _sigmoid(ba)
    xa = ba + dtb_ref[...]
    softplus = jnp.maximum(xa, 0.0) + jnp.log1p(jnp.exp(-jnp.abs(xa)))
    g = -jnp.exp(alog_ref[...]) * softplus

    pair = 2 * DN_HEAD_DIM
    row = lax.broadcasted_iota(jnp.int32, (CHUNK, LANES), 0)
    lane = lax.broadcasted_iota(jnp.int32, (CHUNK, LANES), 1)
    second = lane >= CHUNK
    col = jnp.where(second, lane - CHUNK, lane)
    incl = row >= col
    strict = row > col
    eye = (row == col).astype(F32)
    trow = lax.broadcasted_iota(jnp.int32, (CHUNK, CHUNK), 0)
    tcol = lax.broadcasted_iota(jnp.int32, (CHUNK, CHUNK), 1)
    tril = (trow >= tcol).astype(F32)
    sel = (lax.broadcasted_iota(jnp.int32, (SUBLANES, LANES), 0)
           == lax.broadcasted_iota(jnp.int32, (SUBLANES, LANES), 1)).astype(F32)

    for ci in range(n_chunks):
        gc = _dot(tril, g[ci * CHUNK:(ci + 1) * CHUNK, :], precision=HIGHEST)
        gc_s[ci * CHUNK:(ci + 1) * CHUNK, :] = gc
        nxt = pltpu.roll(gc, LANES - 1, 1)
        gct_s[ci] = lax.dot_general(sel, jnp.concatenate([gc, nxt], axis=0), NT_DIMS,
                                    precision=HIGHEST, preferred_element_type=F32)
        gl_s[ci:ci + 1, :] = jnp.exp(gc[CHUNK - 1:CHUNK, :])

    def block_diag(m):
        half = m.shape[1] // 2
        z = jnp.zeros_like(m[:, :half])
        return jnp.concatenate([jnp.concatenate([m[:, :half], z], axis=1),
                                jnp.concatenate([z, m[:, half:]], axis=1)], axis=0)

    def block_diag_lanes(m):
        return jnp.concatenate([jnp.where(second, 0.0, m), jnp.where(second, m, 0.0)], axis=0)

    def intra_chunk(it, carry):
        streams = [(it * CHUNKS_PER_ITER + sub, p) for sub in range(CHUNKS_PER_ITER)
                   for p in range(DN_HEADS // 2)]
        st = []
        for ci, p in streams:
            rows = pl.ds(pl.multiple_of(ci * CHUNK, CHUNK), CHUNK)
            ps = slice(p * pair, (p + 1) * pair)
            gc = gc_s[rows, :]
            l0 = DECAY_LANE0 + 2 * p
            gcol0, gcol1 = gc[:, l0:l0 + 1], gc[:, l0 + 1:l0 + 2]
            gcol = jnp.where(second, gcol1, gcol0)
            grow = gct_s[ci][l0:l0 + 1, :]
            decay = jnp.where(incl, jnp.exp(jnp.where(incl, gcol - grow, 0.0)), 0.0)
            q, k, v = q_s[rows, ps], k_s[rows, ps], v_s[rows, ps]
            b0 = beta_s[rows, BETA_LANE0 + 2 * p:BETA_LANE0 + 2 * p + 1]
            b1 = beta_s[rows, BETA_LANE0 + 2 * p + 1:BETA_LANE0 + 2 * p + 2]
            kkqk = lax.dot_general(jnp.concatenate([k, q], axis=0).astype(BF16),
                                   block_diag(k.astype(BF16)), NT_DIMS, preferred_element_type=F32)
            st.append(dict(ci=ci, p=p, rows=rows, ps=ps, gc=gc, l0=l0, gcol0=gcol0, gcol1=gcol1,
                           decay=decay, q=q, k=k, v=v, b0=b0, b1=b1, kkqk=kkqk))
        for s in st:
            a = jnp.where(strict, jnp.where(second, s['b1'], s['b0']) * s['kkqk'][:CHUNK] * s['decay'], 0.0)
            s['inv'] = eye - a
            s['pw'] = _dot(a.astype(BF16), block_diag_lanes(a).astype(BF16))
        for _ in range(4):
            for s in st:
                r = _dot(jnp.concatenate([s['inv'], s['pw']], axis=0).astype(BF16),
                         block_diag_lanes(s['pw']).astype(BF16))
                s['inv'] = s['inv'] + r[:CHUNK]
                s['pw'] = r[CHUNK:]
        for s in st:
            s['inv'] = s['inv'] + _dot(s['inv'].astype(BF16), block_diag_lanes(s['pw']).astype(BF16))
        for s in st:
            eg0, eg1 = jnp.exp(s['gcol0']), jnp.exp(s['gcol1'])
            k, v, q = s['k'], s['v'], s['q']
            hd = DN_HEAD_DIM
            rhs = jnp.concatenate([s['b0'] * v[:, :hd], (s['b0'] * eg0) * k[:, :hd],
                                   s['b1'] * v[:, hd:], (s['b1'] * eg1) * k[:, hd:]], axis=1)
            sol = _dot(s['inv'].astype(BF16), block_diag(rhs.astype(BF16)))
            ci, p = s['ci'], s['p']
            u_s[s['rows'], s['ps']] = jnp.concatenate([sol[:, :hd], sol[:, 2 * hd:3 * hd]], axis=1)
            ws_s[ci, p, 0:CHUNK, :] = jnp.concatenate(
                [sol[:, hd:2 * hd], sol[:, 3 * hd:]], axis=1).astype(BF16)
            ws_s[ci, p, CHUNK:2 * CHUNK, :] = jnp.concatenate(
                [q[:, :hd] * eg0, q[:, hd:] * eg1], axis=1).astype(BF16)
            vn_s[ci, p, 0:CHUNK, :] = (s['kkqk'][CHUNK:] * s['decay']).astype(BF16)
            gc, l0 = s['gc'], s['l0']
            glast0 = gc[CHUNK - 1:CHUNK, l0:l0 + 1]
            glast1 = gc[CHUNK - 1:CHUNK, l0 + 1:l0 + 2]
            kdec = jnp.concatenate([k[:, :hd] * jnp.exp(glast0 - s['gcol0']),
                                    k[:, hd:] * jnp.exp(glast1 - s['gcol1'])], axis=0)
            vn_s[ci, p, CHUNK:CHUNK + hd, :] = jnp.transpose(kdec).astype(BF16)
        return carry

    lax.fori_loop(0, n_chunks // CHUNKS_PER_ITER, intra_chunk, 0)

    def inter_chunk(ci, carry):
        rows = pl.ds(pl.multiple_of(ci * CHUNK, CHUNK), CHUNK)
        glrow = gl_s[pl.ds(ci, 1), :]
        hd = DN_HEAD_DIM
        pairs = range(DN_HEADS // 2)
        r1 = [_dot(ws_s[ci, p], block_diag(jnp.concatenate(
            [state_ref[2 * p], state_ref[2 * p + 1]], axis=1).astype(BF16))) for p in pairs]
        v_new = [u_s[rows, p * pair:(p + 1) * pair] - r1[p][:CHUNK] for p in pairs]
        r2 = [_dot(vn_s[ci, p], block_diag(v_new[p].astype(BF16))) for p in pairs]
        for p in pairs:
            o2 = r1[p][CHUNK:] + r2[p][:CHUNK]
            for j in range(2):
                h = 2 * p + j
                hs = slice(h * hd, (h + 1) * hd)
                lane_h = DECAY_LANE0 + h
                state_ref[h] = (state_ref[h] * glrow[:, lane_h:lane_h + 1]
                                + r2[p][CHUNK:, j * hd:(j + 1) * hd])
                o = o2[:, j * hd:(j + 1) * hd]
                o = o * lax.rsqrt(jnp.mean(o * o, axis=-1, keepdims=True) + EPS) * onorm_ref[...]
                y_ref[rows, hs] = (o * _silu(z_ref[rows, hs])).astype(y_ref.dtype)
        return carry

    lax.fori_loop(0, n_chunks, inter_chunk, 0)


def _deltanet(qkv, z, ba, cw, alog, dtb, onorm, batch, tb=512):
    T = qkv.shape[0]
    nb = T // batch // tb
    n_chunks = tb // CHUNK
    full = lambda a: pl.BlockSpec(a.shape, lambda b, i: (0, 0))
    rows = lambda n: pl.BlockSpec((tb, n), lambda b, i: (b * nb + i, 0))
    scr = lambda *s: pltpu.VMEM(s, F32)
    return pl.pallas_call(
        _deltanet_kernel,
        grid=(batch, nb),
        in_specs=[rows(3 * DN_WIDTH), rows(DN_WIDTH), rows(LANES),
                  full(cw), full(alog), full(dtb), full(onorm)],
        out_specs=rows(DN_WIDTH),
        out_shape=jax.ShapeDtypeStruct((T, DN_WIDTH), BF16),
        scratch_shapes=[
            scr(DN_HEADS, DN_HEAD_DIM, DN_HEAD_DIM),
            scr(SUBLANES, 3 * DN_WIDTH),
            scr(SUBLANES + tb, 3 * DN_WIDTH),
            scr(tb, DN_WIDTH), scr(tb, DN_WIDTH), scr(tb, DN_WIDTH),
            scr(tb, LANES),
            scr(tb, LANES), scr(n_chunks, SUBLANES, 2 * CHUNK), scr(n_chunks, LANES),
            scr(tb, DN_WIDTH),
            pltpu.VMEM((n_chunks, DN_HEADS // 2, 2 * CHUNK, 2 * DN_HEAD_DIM), BF16),
            pltpu.VMEM((n_chunks, DN_HEADS // 2, CHUNK + DN_HEAD_DIM, 2 * CHUNK), BF16),
        ],
        compiler_params=_params("parallel", "arbitrary"),
        name="deltanet",
    )(qkv, z, ba, cw, alog, dtb, onorm)


def _conformer_kernel(glu_ref, wdw_ref, bdw_ref, lng_ref, lnb_ref, y_ref, cbuf_ref):
    tc = glu_ref.shape[0]
    halo = 4 * SUBLANES

    @pl.when(pl.program_id(1) == 0)
    def _():
        cbuf_ref[0:halo, :] = jnp.zeros((halo, CONF_WIDTH), F32)

    u = glu_ref[...]
    g = u[:, :CONF_WIDTH] * _sigmoid(u[:, CONF_WIDTH:])
    cbuf_ref[halo:halo + tc, :] = g
    first = halo - (CONF_KERNEL - 1)
    acc = bdw_ref[...]
    for phase in range(SUBLANES):
        taps = [j for j in range(CONF_KERNEL) if (first + j) % SUBLANES == phase]
        n = tc if phase == 0 else tc + SUBLANES
        part = None
        for j in taps:
            r0 = first + j - phase
            term = wdw_ref[j:j + 1, :] * cbuf_ref[r0:r0 + n, :]
            part = term if part is None else part + term
        acc = acc + part[phase:phase + tc, :]
    cbuf_ref[0:halo, :] = cbuf_ref[tc:tc + halo, :]
    mu = jnp.mean(acc, axis=-1, keepdims=True)
    d = acc - mu
    var = jnp.mean(d * d, axis=-1, keepdims=True)
    y = d * lax.rsqrt(var + EPS) * lng_ref[...] + lnb_ref[...]
    y_ref[...] = _silu(y).astype(y_ref.dtype)


def _conformer(glu, wdw, bdw, lng, lnb, batch, tc=512):
    T = glu.shape[0]
    nb = T // batch // tc
    full = lambda a: pl.BlockSpec(a.shape, lambda b, i: (0, 0))
    rows = lambda n: pl.BlockSpec((tc, n), lambda b, i: (b * nb + i, 0))
    return pl.pallas_call(
        _conformer_kernel,
        grid=(batch, nb),
        in_specs=[rows(2 * CONF_WIDTH), full(wdw), full(bdw), full(lng), full(lnb)],
        out_specs=rows(CONF_WIDTH),
        out_shape=jax.ShapeDtypeStruct((T, CONF_WIDTH), BF16),
        scratch_shapes=[pltpu.VMEM((4 * SUBLANES + tc, CONF_WIDTH), F32)],
        compiler_params=_params("parallel", "arbitrary"),
        name="conformer",
    )(glu, wdw, bdw, lng, lnb)


ROUTE_GATE0, ROUTE_GATE1, ROUTE_EXPERT0, ROUTE_EXPERT1, ROUTE_RANK0, ROUTE_RANK1 = range(6)


def _lane_argmax(vals, valid, lane):
    neg = jnp.float32(-1e30)
    m = jnp.max(jnp.where(valid, vals, neg), axis=-1, keepdims=True)
    idx = jnp.min(jnp.where(valid & (vals == m), lane, LANES), axis=-1, keepdims=True)
    return m, idx


def _out_router_kernel(x_ref, ydn_ref, ycf_ref, wo1_ref, wo2_ref, nw_ref, wrh_ref, wrl_ref, br_ref,
                       x2_ref, h_ref, route_ref, plan_ref, cnt_ref, base_ref):
    tm = x_ref.shape[0]

    @pl.when(pl.program_id(0) == 0)
    def _():
        base_ref[...] = jnp.zeros_like(base_ref)

    x2 = x_ref[...] + _dot(ydn_ref[...], wo1_ref[...]) + _dot(ycf_ref[...], wo2_ref[...])
    x2_ref[...] = x2
    h = _rmsnorm(x2, nw_ref[...])
    _store_token_tiles(h_ref, 0, h)
    h_hi = h.astype(BF16)
    h_lo = (h - h_hi.astype(F32)).astype(BF16)
    hh = _dot(jnp.concatenate([h_hi, h_lo], axis=0), wrh_ref[...])
    logits = hh[:tm] + hh[tm:] + _dot(h_hi, wrl_ref[...]) + br_ref[...]

    lane = lax.broadcasted_iota(jnp.int32, (tm, LANES), 1)
    gvalid = (lane >= GROUP_LANE0) & (lane < GROUP_LANE0 + N_GROUPS)
    gmax, _ = _lane_argmax(logits, gvalid, lane)
    gexp = jnp.where(gvalid, jnp.exp(jnp.where(gvalid, logits - gmax, 0.0)), 0.0)
    p_grp = gexp / jnp.sum(gexp, axis=-1, keepdims=True)
    p_g, g_lane = _lane_argmax(p_grp, gvalid, lane)
    g_sel = g_lane - GROUP_LANE0
    evalid = (lane < N_EXPERTS) & ((lane // EXPERTS_PER_GROUP) == g_sel)
    emax, _ = _lane_argmax(logits, evalid, lane)
    eexp = jnp.where(evalid, jnp.exp(jnp.where(evalid, logits - emax, 0.0)), 0.0)
    p_e = eexp / jnp.sum(eexp, axis=-1, keepdims=True)
    p1, e1 = _lane_argmax(p_e, evalid, lane)
    p2, e2 = _lane_argmax(p_e, evalid & (lane != e1), lane)
    psum = p1 + p2
    gate1 = p_g * p1 / psum
    gate2 = p_g * p2 / psum

    hot1 = (lane == e1).astype(F32)
    hot2 = (lane == e2).astype(F32)
    hot = hot1 + hot2
    trow = lax.broadcasted_iota(jnp.int32, (tm, tm), 0)
    tcol = lax.broadcasted_iota(jnp.int32, (tm, tm), 1)
    before = (trow > tcol).astype(BF16)
    seen = base_ref[...] + _dot(before, hot.astype(BF16))
    rank1 = jnp.sum(hot1 * seen, axis=-1, keepdims=True)
    rank2 = jnp.sum(hot2 * seen, axis=-1, keepdims=True)
    base_ref[...] = base_ref[...] + jnp.sum(hot, axis=0, keepdims=True)
    cnt_ref[...] = base_ref[...]

    route = jnp.zeros((tm, LANES), F32)
    for pos, val in ((ROUTE_GATE0, gate1), (ROUTE_GATE1, gate2),
                     (ROUTE_EXPERT0, e1.astype(F32)), (ROUTE_EXPERT1, e2.astype(F32)),
                     (ROUTE_RANK0, rank1), (ROUTE_RANK1, rank2)):
        route = jnp.where(lane == pos, val, route)
    route_ref[...] = route
    plan_ref[...] = jnp.transpose(route)[:SUBLANES, :]


def _out_router(x, ydn, ycf, wo1, wo2, nw, wr, br, tm=512):
    T, D = x.shape
    full = lambda a: pl.BlockSpec(a.shape, lambda i: (0, 0), pipeline_mode=pl.Buffered(1))
    rows = lambda n: pl.BlockSpec((tm, n), lambda i: (i, 0))
    wr_hi = wr.astype(BF16)
    wr_lo = (wr - wr_hi.astype(F32)).astype(BF16)
    return pl.pallas_call(
        _out_router_kernel,
        grid=(T // tm,),
        in_specs=[rows(D), rows(DN_WIDTH), rows(CONF_WIDTH), full(wo1), full(wo2), full(nw),
                  full(wr_hi), full(wr_lo), full(br)],
        out_specs=[rows(D), pl.BlockSpec((tm * TOKEN_TILE, LANES), lambda i: (i, 0)), rows(LANES),
                   pl.BlockSpec((SUBLANES, tm), lambda i: (0, i)),
                   pl.BlockSpec((1, LANES), lambda i: (0, 0))],
        out_shape=[jax.ShapeDtypeStruct((T, D), F32),
                   jax.ShapeDtypeStruct((T * TOKEN_TILE, LANES), F32),
                   jax.ShapeDtypeStruct((T, LANES), F32), jax.ShapeDtypeStruct((SUBLANES, T), F32),
                   jax.ShapeDtypeStruct((1, LANES), F32)],
        scratch_shapes=[pltpu.VMEM((1, LANES), F32)],
        compiler_params=_params("arbitrary"),
        name="out_router",
    )(x, ydn, ycf, wo1, wo2, nw, wr_hi, wr_lo, br)


def _experts_kernel(tok_ref, be_ref, nused_ref, h_ref, wg_ref, wu_ref, wd_ref, ys_ref,
                    buf_a, buf_b, sem_a, sem_b, wg_s, wu_s, wd_s):
    i = pl.program_id(0)
    n_used = nused_ref[0]

    def gather(block, half, buf, sem):
        base = block * SLOT_BLOCK + half * GATHER_ROWS
        for r in range(GATHER_ROWS):
            pltpu.make_async_copy(h_ref.at[_tile_rows(tok_ref[base + r])],
                                  buf.at[pl.ds(r * TOKEN_TILE, TOKEN_TILE)], sem).start(priority=r % 2)

    def gathered(buf, sem):
        pltpu.make_async_copy(h_ref.at[pl.ds(0, GATHER_ROWS * TOKEN_TILE)], buf, sem).wait()

    def ffn(buf, half):
        xb = _load_token_tiles(buf, GATHER_ROWS).astype(BF16)
        act = _silu(_dot(xb, wg_s[...])) * _dot(xb, wu_s[...])
        _store_token_tiles(ys_ref, half * GATHER_ROWS, _dot(act.astype(BF16), wd_s[...]))

    @pl.when(i < n_used)
    def _():
        @pl.when(i == 0)
        def _():
            gather(0, 0, buf_a, sem_a)

        changed = jnp.logical_or(i == 0, be_ref[i] != be_ref[jnp.maximum(i - 1, 0)])

        @pl.when(changed)
        def _():
            wg_s[...] = wg_ref[0, 0].astype(BF16)
            wu_s[...] = wu_ref[0, 0].astype(BF16)
            wd_s[...] = wd_ref[0, 0].astype(BF16)

        gathered(buf_a, sem_a)
        gather(i, 1, buf_b, sem_b)
        ffn(buf_a, 0)
        gathered(buf_b, sem_b)
        gather(jnp.minimum(i + 1, n_used - 1), 0, buf_a, sem_a)
        ffn(buf_b, 1)

        @pl.when(i + 1 >= n_used)
        def _():
            gathered(buf_a, sem_a)

    @pl.when(i >= n_used)
    def _():
        ys_ref[...] = jnp.zeros_like(ys_ref)


def _experts(slot_tok, block_expert, n_used, h, w_gate, w_up, w_down, layer):
    D = h.shape[1] * TOKEN_TILE
    P = slot_tok.shape[0]
    wsel = lambda i, tok, be, nu: (layer, be[i], 0, 0)
    return pl.pallas_call(
        _experts_kernel,
        grid_spec=pltpu.PrefetchScalarGridSpec(
            num_scalar_prefetch=3,
            grid=(P // SLOT_BLOCK,),
            in_specs=[pl.BlockSpec(memory_space=pl.ANY),
                      pl.BlockSpec((1, 1, D, D_EXPERT), wsel),
                      pl.BlockSpec((1, 1, D, D_EXPERT), wsel),
                      pl.BlockSpec((1, 1, D_EXPERT, D), wsel)],
            out_specs=pl.BlockSpec((SLOT_BLOCK * TOKEN_TILE, LANES), lambda i, tok, be, nu: (i, 0)),
            scratch_shapes=[pltpu.VMEM((GATHER_ROWS * TOKEN_TILE, LANES), F32),
                            pltpu.VMEM((GATHER_ROWS * TOKEN_TILE, LANES), F32),
                            pltpu.SemaphoreType.DMA(()), pltpu.SemaphoreType.DMA(()),
                            pltpu.VMEM((D, D_EXPERT), BF16), pltpu.VMEM((D, D_EXPERT), BF16),
                            pltpu.VMEM((D_EXPERT, D), BF16)],
        ),
        out_shape=jax.ShapeDtypeStruct((P * TOKEN_TILE, LANES), F32),
        compiler_params=_params("arbitrary"),
        name="experts",
    )(slot_tok, block_expert, n_used, h, w_gate, w_up, w_down)


def _combine_kernel(dest_ref, x_ref, route_ref, nw_ref, ys_ref, o_ref, gbuf_ref, sem, *, final_norm):
    tm = x_ref.shape[0]
    base = pl.program_id(0) * tm

    def row_copy(r, k):
        return pltpu.make_async_copy(ys_ref.at[_tile_rows(dest_ref[2 * (base + r) + k])],
                                     gbuf_ref.at[k, _tile_rows(r)], sem)

    def start(r, carry):
        row_copy(r, 0).start(priority=0)
        row_copy(r, 1).start(priority=1)
        return carry

    lax.fori_loop(0, tm, start, 0)
    for k in range(2):
        pltpu.make_async_copy(ys_ref.at[pl.ds(0, tm * TOKEN_TILE)], gbuf_ref.at[k], sem).wait()
    route = route_ref[...]
    g0 = route[:, ROUTE_GATE0:ROUTE_GATE0 + 1]
    g1 = route[:, ROUTE_GATE1:ROUTE_GATE1 + 1]
    x = x_ref[...] + (_load_token_tiles(gbuf_ref.at[0], tm) * g0 + _load_token_tiles(gbuf_ref.at[1], tm) * g1)
    o_ref[...] = _rmsnorm(x, nw_ref[...]) if final_norm else x


def _combine(dest, x, route, nw, ys, final_norm, tm=256):
    T, D = x.shape
    return pl.pallas_call(
        functools.partial(_combine_kernel, final_norm=final_norm),
        grid_spec=pltpu.PrefetchScalarGridSpec(
            num_scalar_prefetch=1,
            grid=(T // tm,),
            in_specs=[pl.BlockSpec((tm, D), lambda i, d: (i, 0)),
                      pl.BlockSpec((tm, LANES), lambda i, d: (i, 0)),
                      pl.BlockSpec((1, D), lambda i, d: (0, 0)),
                      pl.BlockSpec(memory_space=pl.ANY)],
            out_specs=pl.BlockSpec((tm, D), lambda i, d: (i, 0)),
            scratch_shapes=[pltpu.VMEM((2, tm * TOKEN_TILE, LANES), F32), pltpu.SemaphoreType.DMA(())],
        ),
        out_shape=jax.ShapeDtypeStruct((T, D), F32),
        compiler_params=_params("arbitrary"),
        name="combine",
    )(dest, x, route, nw, ys)


def _slot_tokens_kernel(dest_ref, tok_ref):
    n_tokens = dest_ref.shape[0] // 2

    def clear(i, carry):
        tok_ref[i] = 0
        return carry

    def place(t, carry):
        tok_ref[dest_ref[t]] = t
        tok_ref[dest_ref[n_tokens + t]] = t
        return carry

    lax.fori_loop(0, tok_ref.shape[0], clear, 0, unroll=8)
    lax.fori_loop(0, n_tokens, place, 0, unroll=8)


def _slot_tokens(dest, n_slots):
    return pl.pallas_call(
        _slot_tokens_kernel,
        in_specs=[pl.BlockSpec(memory_space=pltpu.SMEM)],
        out_specs=pl.BlockSpec(memory_space=pltpu.SMEM),
        out_shape=jax.ShapeDtypeStruct((n_slots,), jnp.int32),
        name="slot_tokens",
    )(dest)


def _slot_plan(plan, counts_row, n_tokens):
    n_blocks = 2 * n_tokens // SLOT_BLOCK + N_EXPERTS
    counts = counts_row[0, :N_EXPERTS].astype(jnp.int32)
    padded = (counts + SLOT_BLOCK - 1) // SLOT_BLOCK * SLOT_BLOCK
    pad_ends = jnp.cumsum(padded)
    pad_starts = pad_ends - padded
    expert = plan[ROUTE_EXPERT0:ROUTE_EXPERT1 + 1].astype(jnp.int32)
    rank = plan[ROUTE_RANK0:ROUTE_RANK1 + 1].astype(jnp.int32)
    hot = expert[..., None] == jnp.arange(N_EXPERTS, dtype=jnp.int32)
    dest = jnp.sum(jnp.where(hot, pad_starts, 0), axis=-1) + rank
    slot_tok = _slot_tokens(dest.reshape(-1), n_blocks * SLOT_BLOCK)
    n_used = pad_ends[-1] // SLOT_BLOCK
    first_row = jnp.minimum(jnp.arange(n_blocks, dtype=jnp.int32), n_used - 1) * SLOT_BLOCK
    block_expert = jnp.sum(pad_ends[None, :] <= first_row[:, None], axis=1).astype(jnp.int32)
    block_expert = jnp.minimum(block_expert, N_EXPERTS - 1)
    return dest.T.reshape(-1), slot_tok, block_expert, n_used.astype(jnp.int32).reshape(1)


def _lane_pad(a, lane0, width=LANES):
    return jnp.pad(a.astype(F32), ((0, 0), (lane0, width - lane0 - a.shape[1])))


def kernel(x, norm_mix, w_in, conv_qkv, a_log, dt_bias, o_norm, w_dw, b_dw, ln_g, ln_b,
           w_out, norm_ffn, w_rg, b_rg, w_re, b_re, w_gate, w_up, w_down, norm_final):
    batch, seq, d_model = x.shape
    T = batch * seq
    depth = w_in.shape[0]
    o_z = 3 * DN_WIDTH
    o_b = 4 * DN_WIDTH
    o_c = o_b + 2 * DN_HEADS
    xt = x.reshape(T, d_model)
    for l in range(depth):
        wi = w_in[l]
        qkv, z, ba, glu = _in_proj(
            xt, norm_mix[l][None, :], wi[:, :o_z].astype(BF16), wi[:, o_z:o_b].astype(BF16),
            _lane_pad(wi[:, o_b:o_c], 0).astype(BF16), wi[:, o_c:].astype(BF16))
        y_dn = _deltanet(qkv, z, ba, conv_qkv[l], _lane_pad(a_log[l][None, :], DECAY_LANE0),
                         _lane_pad(dt_bias[l][None, :], DECAY_LANE0), o_norm[l][None, :], batch)
        y_cf = _conformer(glu, w_dw[l], b_dw[l][None, :], ln_g[l][None, :], ln_b[l][None, :], batch)
        wr = jnp.concatenate([w_re[l], _lane_pad(w_rg[l], 0, LANES - N_EXPERTS)], axis=1)
        br = jnp.concatenate([b_re[l][None, :], _lane_pad(b_rg[l][None, :], 0, LANES - N_EXPERTS)], axis=1)
        x2, h, route, plan, counts = _out_router(
            xt, y_dn, y_cf, w_out[l][:DN_WIDTH].astype(BF16), w_out[l][DN_WIDTH:].astype(BF16),
            norm_ffn[l][None, :], wr, br)
        dest, slot_tok, block_expert, n_used = _slot_plan(plan, counts, T)
        ys = _experts(slot_tok, block_expert, n_used, h, w_gate, w_up, w_down, l)
        last = l == depth - 1
        xt = _combine(dest, x2, route, norm_final[None, :] if last else norm_ffn[l][None, :],
                      ys, final_norm=last)
    return xt.reshape(batch, seq, d_model)
```

```python
import functools

import jax
import jax.numpy as jnp
from jax import lax
from jax.experimental import pallas as pl
from jax.experimental.pallas import tpu as pltpu

EPS = 1e-6
CHUNK = 64
DN_HEADS = 4
DN_HEAD_DIM = 128
DN_WIDTH = DN_HEADS * DN_HEAD_DIM
SHORT_CONV = 4
CONF_WIDTH = 512
CONF_KERNEL = 31
N_GROUPS = 4
EXPERTS_PER_GROUP = 8
N_EXPERTS = N_GROUPS * EXPERTS_PER_GROUP
D_EXPERT = 512
LANES = 128
SUBLANES = 8
GROUP_LANE0 = N_EXPERTS
BETA_LANE0 = 0
DECAY_LANE0 = DN_HEADS
SLOT_BLOCK = 256
TOKEN_TILE = 8
CHUNKS_PER_ITER = 4
VMEM_LIMIT = 56 * 1024 * 1024

F32 = jnp.float32
BF16 = jnp.bfloat16
HIGHEST = lax.Precision.HIGHEST
NT_DIMS = (((1,), (1,)), ((), ()))
TN_DIMS = (((0,), (0,)), ((), ()))


def _params(*semantics):
    return pltpu.CompilerParams(dimension_semantics=semantics, vmem_limit_bytes=VMEM_LIMIT)


def _sigmoid(x):
    return 1.0 / (1.0 + jnp.exp(-x))


def _silu(x):
    return x * _sigmoid(x)


def _rmsnorm(x, w):
    return x * lax.rsqrt(jnp.mean(x * x, axis=-1, keepdims=True) + EPS) * w


def _dot(a, b, **kw):
    return jnp.dot(a, b, preferred_element_type=F32, **kw)


def _store_token_tiles(ref, row0, x):
    n = x.shape[0]
    for j in range(TOKEN_TILE):
        ref[pl.ds(row0 * TOKEN_TILE + j, n, stride=TOKEN_TILE), :] = x[:, j * LANES:(j + 1) * LANES]


def _load_token_tiles(ref, n):
    return jnp.concatenate([ref[pl.ds(j, n, stride=TOKEN_TILE), :] for j in range(TOKEN_TILE)], axis=1)


def _tile_rows(row):
    return pl.ds(pl.multiple_of(row * TOKEN_TILE, TOKEN_TILE), TOKEN_TILE)


def _in_proj_kernel(x_ref, nw_ref, wqkv_ref, wz_ref, wba_ref, wglu_ref,
                    qkv_ref, z_ref, ba_ref, glu_ref):
    hb = _rmsnorm(x_ref[...], nw_ref[...]).astype(BF16)
    qkv_ref[...] = _dot(hb, wqkv_ref[...])
    z_ref[...] = _dot(hb, wz_ref[...])
    ba_ref[...] = _dot(hb, wba_ref[...])
    glu_ref[...] = _dot(hb, wglu_ref[...])


def _in_proj(x, nw, wqkv, wz, wba, wglu, tm=512):
    T, D = x.shape
    full = lambda a: pl.BlockSpec(a.shape, lambda i: (0, 0), pipeline_mode=pl.Buffered(1))
    rows = lambda n: pl.BlockSpec((tm, n), lambda i: (i, 0))
    widths = (wqkv.shape[1], wz.shape[1], wba.shape[1], wglu.shape[1])
    return pl.pallas_call(
        _in_proj_kernel,
        grid=(T // tm,),
        in_specs=[rows(D), full(nw), full(wqkv), full(wz), full(wba), full(wglu)],
        out_specs=[rows(n) for n in widths],
        out_shape=[jax.ShapeDtypeStruct((T, n), F32) for n in widths],
        compiler_params=_params("parallel"),
        name="in_proj",
    )(x, nw, wqkv, wz, wba, wglu)


def _deltanet_kernel(qkv_ref, z_ref, ba_ref, cw_ref, alog_ref, dtb_ref, onorm_ref, y_ref,
                     state_ref, tail_ref, cbuf_ref, q_s, k_s, v_s, beta_s, gc_s, gct_s, gl_s,
                     u_s, ws_s, vn_s):
    tb = qkv_ref.shape[0]
    n_chunks = tb // CHUNK
    halo = SUBLANES

    @pl.when(pl.program_id(1) == 0)
    def _():
        state_ref[...] = jnp.zeros_like(state_ref)
        tail_ref[...] = jnp.zeros_like(tail_ref)

    cbuf_ref[0:halo, :] = tail_ref[...]
    cbuf_ref[halo:halo + tb, :] = qkv_ref[...]
    tail_ref[...] = qkv_ref[tb - halo:tb, :]
    first = halo - (SHORT_CONV - 1)
    acc = cw_ref[0:1, :] * cbuf_ref[first:first + tb, :]
    for j in range(1, SHORT_CONV):
        acc = acc + cw_ref[j:j + 1, :] * cbuf_ref[first + j:first + j + tb, :]
    c = _silu(acc)

    for h in range(DN_HEADS):
        hs = slice(h * DN_HEAD_DIM, (h + 1) * DN_HEAD_DIM)
        q = c[:, h * DN_HEAD_DIM:(h + 1) * DN_HEAD_DIM]
        k = c[:, DN_WIDTH + h * DN_HEAD_DIM:DN_WIDTH + (h + 1) * DN_HEAD_DIM]
        q_s[:, hs] = q * lax.rsqrt(jnp.sum(q * q, axis=-1, keepdims=True) + EPS) * (DN_HEAD_DIM ** -0.5)
        k_s[:, hs] = k * lax.rsqrt(jnp.sum(k * k, axis=-1, keepdims=True) + EPS)
    v_s[...] = c[:, 2 * DN_WIDTH:3 * DN_WIDTH]

    ba = ba_ref[...]
    beta_s[...] = _sigmoid(ba)
    xa = ba + dtb_ref[...]
    softplus = jnp.maximum(xa, 0.0) + jnp.log1p(jnp.exp(-jnp.abs(xa)))
    g = -jnp.exp(alog_ref[...]) * softplus

    pair = 2 * DN_HEAD_DIM
    row = lax.broadcasted_iota(jnp.int32, (CHUNK, LANES), 0)
    lane = lax.broadcasted_iota(jnp.int32, (CHUNK, LANES), 1)
    second = lane >= CHUNK
    col = jnp.where(second, lane - CHUNK, lane)
    incl = row >= col
    strict = row > col
    eye = (row == col).astype(F32)
    trow = lax.broadcasted_iota(jnp.int32, (CHUNK, CHUNK), 0)
    tcol = lax.broadcasted_iota(jnp.int32, (CHUNK, CHUNK), 1)
    tril = (trow >= tcol).astype(F32)
    sel = (lax.broadcasted_iota(jnp.int32, (SUBLANES, LANES), 0)
           == lax.broadcasted_iota(jnp.int32, (SUBLANES, LANES), 1)).astype(F32)

    for ci in range(n_chunks):
        gc = _dot(tril, g[ci * CHUNK:(ci + 1) * CHUNK, :], precision=HIGHEST)
        gc_s[ci * CHUNK:(ci + 1) * CHUNK, :] = gc
        nxt = pltpu.roll(gc, LANES - 1, 1)
        gct_s[ci] = lax.dot_general(sel, jnp.concatenate([gc, nxt], axis=0), NT_DIMS,
                                    precision=HIGHEST, preferred_element_type=F32)
        gl_s[ci:ci + 1, :] = jnp.exp(gc[CHUNK - 1:CHUNK, :])

    def block_diag(m):
        half = m.shape[1] // 2
        z = jnp.zeros_like(m[:, :half])
        return jnp.concatenate([jnp.concatenate([m[:, :half], z], axis=1),
                                jnp.concatenate([z, m[:, half:]], axis=1)], axis=0)

    def block_diag_lanes(m):
        return jnp.concatenate([jnp.where(second, 0.0, m), jnp.where(second, m, 0.0)], axis=0)

    def intra_chunk(it, carry):
        streams = [(it * CHUNKS_PER_ITER + sub, p) for sub in range(CHUNKS_PER_ITER)
                   for p in range(DN_HEADS // 2)]
        st = []
        for ci, p in streams:
            rows = pl.ds(pl.multiple_of(ci * CHUNK, CHUNK), CHUNK)
            ps = slice(p * pair, (p + 1) * pair)
            gc = gc_s[rows, :]
            l0 = DECAY_LANE0 + 2 * p
            gcol0, gcol1 = gc[:, l0:l0 + 1], gc[:, l0 + 1:l0 + 2]
            gcol = jnp.where(second, gcol1, gcol0)
            grow = gct_s[ci][l0:l0 + 1, :]
            decay = jnp.where(incl, jnp.exp(jnp.where(incl, gcol - grow, 0.0)), 0.0)
            q, k, v = q_s[rows, ps], k_s[rows, ps], v_s[rows, ps]
            b0 = beta_s[rows, BETA_LANE0 + 2 * p:BETA_LANE0 + 2 * p + 1]
            b1 = beta_s[rows, BETA_LANE0 + 2 * p + 1:BETA_LANE0 + 2 * p + 2]
            kkqk = lax.dot_general(jnp.concatenate([k, q], axis=0).astype(BF16),
                                   block_diag(k.astype(BF16)), NT_DIMS, preferred_element_type=F32)
            st.append(dict(ci=ci, p=p, rows=rows, ps=ps, gc=gc, l0=l0, gcol0=gcol0, gcol1=gcol1,
                           decay=decay, q=q, k=k, v=v, b0=b0, b1=b1, kkqk=kkqk))
        for s in st:
            a = jnp.where(strict, jnp.where(second, s['b1'], s['b0']) * s['kkqk'][:CHUNK] * s['decay'], 0.0)
            s['inv'] = eye - a
            s['pw'] = _dot(a.astype(BF16), block_diag_lanes(a).astype(BF16))
        for _ in range(4):
            for s in st:
                r = _dot(jnp.concatenate([s['inv'], s['pw']], axis=0).astype(BF16),
                         block_diag_lanes(s['pw']).astype(BF16))
                s['inv'] = s['inv'] + r[:CHUNK]
                s['pw'] = r[CHUNK:]
        for s in st:
            s['inv'] = s['inv'] + _dot(s['inv'].astype(BF16), block_diag_lanes(s['pw']).astype(BF16))
        for s in st:
            eg0, eg1 = jnp.exp(s['gcol0']), jnp.exp(s['gcol1'])
            k, v, q = s['k'], s['v'], s['q']
            hd = DN_HEAD_DIM
            rhs = jnp.concatenate([s['b0'] * v[:, :hd], (s['b0'] * eg0) * k[:, :hd],
                                   s['b1'] * v[:, hd:], (s['b1'] * eg1) * k[:, hd:]], axis=1)
            sol = _dot(s['inv'].astype(BF16), block_diag(rhs.astype(BF16)))
            ci, p = s['ci'], s['p']
            u_s[s['rows'], s['ps']] = jnp.concatenate([sol[:, :hd], sol[:, 2 * hd:3 * hd]], axis=1)
            ws_s[ci, p, 0:CHUNK, :] = jnp.concatenate(
                [sol[:, hd:2 * hd], sol[:, 3 * hd:]], axis=1).astype(BF16)
            ws_s[ci, p, CHUNK:2 * CHUNK, :] = jnp.concatenate(
                [q[:, :hd] * eg0, q[:, hd:] * eg1], axis=1).astype(BF16)
            vn_s[ci, p, 0:CHUNK, :] = (s['kkqk'][CHUNK:] * s['decay']).astype(BF16)
            gc, l0 = s['gc'], s['l0']
            glast0 = gc[CHUNK - 1:CHUNK, l0:l0 + 1]
            glast1 = gc[CHUNK - 1:CHUNK, l0 + 1:l0 + 2]
            kdec = jnp.concatenate([k[:, :hd] * jnp.exp(glast0 - s['gcol0']),
                                    k[:, hd:] * jnp.exp(glast1 - s['gcol1'])], axis=0)
            vn_s[ci, p, CHUNK:CHUNK + hd, :] = jnp.transpose(kdec).astype(BF16)
        return carry

    lax.fori_loop(0, n_chunks // CHUNKS_PER_ITER, intra_chunk, 0)

    def inter_chunk(ci, carry):
        rows = pl.ds(pl.multiple_of(ci * CHUNK, CHUNK), CHUNK)
        glrow = gl_s[pl.ds(ci, 1), :]
        hd = DN_HEAD_DIM
        pairs = range(DN_HEADS // 2)
        r1 = [_dot(ws_s[ci, p], block_diag(jnp.concatenate(
            [state_ref[2 * p], state_ref[2 * p + 1]], axis=1).astype(BF16))) for p in pairs]
        v_new = [u_s[rows, p * pair:(p + 1) * pair] - r1[p][:CHUNK] for p in pairs]
        r2 = [_dot(vn_s[ci, p], block_diag(v_new[p].astype(BF16))) for p in pairs]
        for p in pairs:
            o2 = r1[p][CHUNK:] + r2[p][:CHUNK]
            for j in range(2):
                h = 2 * p + j
                hs = slice(h * hd, (h + 1) * hd)
                lane_h = DECAY_LANE0 + h
                state_ref[h] = (state_ref[h] * glrow[:, lane_h:lane_h + 1]
                                + r2[p][CHUNK:, j * hd:(j + 1) * hd])
                o = o2[:, j * hd:(j + 1) * hd]
                o = o * lax.rsqrt(jnp.mean(o * o, axis=-1, keepdims=True) + EPS) * onorm_ref[...]
                y_ref[rows, hs] = (o * _silu(z_ref[rows, hs])).astype(y_ref.dtype)
        return carry

    lax.fori_loop(0, n_chunks, inter_chunk, 0)


def _deltanet(qkv, z, ba, cw, alog, dtb, onorm, batch, tb=512):
    T = qkv.shape[0]
    nb = T // batch // tb
    n_chunks = tb // CHUNK
    full = lambda a: pl.BlockSpec(a.shape, lambda b, i: (0, 0))
    rows = lambda n: pl.BlockSpec((tb, n), lambda b, i: (b * nb + i, 0))
    scr = lambda *s: pltpu.VMEM(s, F32)
    return pl.pallas_call(
        _deltanet_kernel,
        grid=(batch, nb),
        in_specs=[rows(3 * DN_WIDTH), rows(DN_WIDTH), rows(LANES),
                  full(cw), full(alog), full(dtb), full(onorm)],
        out_specs=rows(DN_WIDTH),
        out_shape=jax.ShapeDtypeStruct((T, DN_WIDTH), BF16),
        scratch_shapes=[
            scr(DN_HEADS, DN_HEAD_DIM, DN_HEAD_DIM),
            scr(SUBLANES, 3 * DN_WIDTH),
            scr(SUBLANES + tb, 3 * DN_WIDTH),
            scr(tb, DN_WIDTH), scr(tb, DN_WIDTH), scr(tb, DN_WIDTH),
            scr(tb, LANES),
            scr(tb, LANES), scr(n_chunks, SUBLANES, 2 * CHUNK), scr(n_chunks, LANES),
            scr(tb, DN_WIDTH),
            pltpu.VMEM((n_chunks, DN_HEADS // 2, 2 * CHUNK, 2 * DN_HEAD_DIM), BF16),
            pltpu.VMEM((n_chunks, DN_HEADS // 2, CHUNK + DN_HEAD_DIM, 2 * CHUNK), BF16),
        ],
        compiler_params=_params("parallel", "arbitrary"),
        name="deltanet",
    )(qkv, z, ba, cw, alog, dtb, onorm)


def _conformer_kernel(glu_ref, wdw_ref, bdw_ref, lng_ref, lnb_ref, y_ref, cbuf_ref):
    tc = glu_ref.shape[0]
    halo = 4 * SUBLANES

    @pl.when(pl.program_id(1) == 0)
    def _():
        cbuf_ref[0:halo, :] = jnp.zeros((halo, CONF_WIDTH), F32)

    u = glu_ref[...]
    g = u[:, :CONF_WIDTH] * _sigmoid(u[:, CONF_WIDTH:])
    cbuf_ref[halo:halo + tc, :] = g
    first = halo - (CONF_KERNEL - 1)
    acc = bdw_ref[...]
    for phase in range(SUBLANES):
        taps = [j for j in range(CONF_KERNEL) if (first + j) % SUBLANES == phase]
        n = tc if phase == 0 else tc + SUBLANES
        part = None
        for j in taps:
            r0 = first + j - phase
            term = wdw_ref[j:j + 1, :] * cbuf_ref[r0:r0 + n, :]
            part = term if part is None else part + term
        acc = acc + part[phase:phase + tc, :]
    cbuf_ref[0:halo, :] = cbuf_ref[tc:tc + halo, :]
    mu = jnp.mean(acc, axis=-1, keepdims=True)
    d = acc - mu
    var = jnp.mean(d * d, axis=-1, keepdims=True)
    y = d * lax.rsqrt(var + EPS) * lng_ref[...] + lnb_ref[...]
    y_ref[...] = _silu(y).astype(y_ref.dtype)


def _conformer(glu, wdw, bdw, lng, lnb, batch, tc=512):
    T = glu.shape[0]
    nb = T // batch // tc
    full = lambda a: pl.BlockSpec(a.shape, lambda b, i: (0, 0))
    rows = lambda n: pl.BlockSpec((tc, n), lambda b, i: (b * nb + i, 0))
    return pl.pallas_call(
        _conformer_kernel,
        grid=(batch, nb),
        in_specs=[rows(2 * CONF_WIDTH), full(wdw), full(bdw), full(lng), full(lnb)],
        out_specs=rows(CONF_WIDTH),
        out_shape=jax.ShapeDtypeStruct((T, CONF_WIDTH), BF16),
        scratch_shapes=[pltpu.VMEM((4 * SUBLANES + tc, CONF_WIDTH), F32)],
        compiler_params=_params("parallel", "arbitrary"),
        name="conformer",
    )(glu, wdw, bdw, lng, lnb)


ROUTE_GATE0, ROUTE_GATE1, ROUTE_EXPERT0, ROUTE_EXPERT1, ROUTE_RANK0, ROUTE_RANK1 = range(6)


def _lane_argmax(vals, valid, lane):
    neg = jnp.float32(-1e30)
    m = jnp.max(jnp.where(valid, vals, neg), axis=-1, keepdims=True)
    idx = jnp.min(jnp.where(valid & (vals == m), lane, LANES), axis=-1, keepdims=True)
    return m, idx


def _out_router_kernel(x_ref, ydn_ref, ycf_ref, wo1_ref, wo2_ref, nw_ref, wrh_ref, wrl_ref, br_ref,
                       x2_ref, h_ref, route_ref, plan_ref, cnt_ref, base_ref):
    tm = x_ref.shape[0]

    @pl.when(pl.program_id(0) == 0)
    def _():
        base_ref[...] = jnp.zeros_like(base_ref)

    x2 = x_ref[...] + _dot(ydn_ref[...], wo1_ref[...]) + _dot(ycf_ref[...], wo2_ref[...])
    x2_ref[...] = x2
    h = _rmsnorm(x2, nw_ref[...])
    _store_token_tiles(h_ref, 0, h)
    h_hi = h.astype(BF16)
    h_lo = (h - h_hi.astype(F32)).astype(BF16)
    hh = _dot(jnp.concatenate([h_hi, h_lo], axis=0), wrh_ref[...])
    logits = hh[:tm] + hh[tm:] + _dot(h_hi, wrl_ref[...]) + br_ref[...]

    lane = lax.broadcasted_iota(jnp.int32, (tm, LANES), 1)
    gvalid = (lane >= GROUP_LANE0) & (lane < GROUP_LANE0 + N_GROUPS)
    gmax, _ = _lane_argmax(logits, gvalid, lane)
    gexp = jnp.where(gvalid, jnp.exp(jnp.where(gvalid, logits - gmax, 0.0)), 0.0)
    p_grp = gexp / jnp.sum(gexp, axis=-1, keepdims=True)
    p_g, g_lane = _lane_argmax(p_grp, gvalid, lane)
    g_sel = g_lane - GROUP_LANE0
    evalid = (lane < N_EXPERTS) & ((lane // EXPERTS_PER_GROUP) == g_sel)
    emax, _ = _lane_argmax(logits, evalid, lane)
    eexp = jnp.where(evalid, jnp.exp(jnp.where(evalid, logits - emax, 0.0)), 0.0)
    p_e = eexp / jnp.sum(eexp, axis=-1, keepdims=True)
    p1, e1 = _lane_argmax(p_e, evalid, lane)
    p2, e2 = _lane_argmax(p_e, evalid & (lane != e1), lane)
    psum = p1 + p2
    gate1 = p_g * p1 / psum
    gate2 = p_g * p2 / psum

    hot1 = (lane == e1).astype(F32)
    hot2 = (lane == e2).astype(F32)
    hot = hot1 + hot2
    trow = lax.broadcasted_iota(jnp.int32, (tm, tm), 0)
    tcol = lax.broadcasted_iota(jnp.int32, (tm, tm), 1)
    before = (trow > tcol).astype(BF16)
    seen = base_ref[...] + _dot(before, hot.astype(BF16))
    rank1 = jnp.sum(hot1 * seen, axis=-1, keepdims=True)
    rank2 = jnp.sum(hot2 * seen, axis=-1, keepdims=True)
    base_ref[...] = base_ref[...] + jnp.sum(hot, axis=0, keepdims=True)
    cnt_ref[...] = base_ref[...]

    route = jnp.zeros((tm, LANES), F32)
    for pos, val in ((ROUTE_GATE0, gate1), (ROUTE_GATE1, gate2),
                     (ROUTE_EXPERT0, e1.astype(F32)), (ROUTE_EXPERT1, e2.astype(F32)),
                     (ROUTE_RANK0, rank1), (ROUTE_RANK1, rank2)):
        route = jnp.where(lane == pos, val, route)
    route_ref[...] = route
    plan_ref[...] = jnp.transpose(route)[:SUBLANES, :]


def _out_router(x, ydn, ycf, wo1, wo2, nw, wr, br, tm=512):
    T, D = x.shape
    full = lambda a: pl.BlockSpec(a.shape, lambda i: (0, 0), pipeline_mode=pl.Buffered(1))
    rows = lambda n: pl.BlockSpec((tm, n), lambda i: (i, 0))
    wr_hi = wr.astype(BF16)
    wr_lo = (wr - wr_hi.astype(F32)).astype(BF16)
    return pl.pallas_call(
        _out_router_kernel,
        grid=(T // tm,),
        in_specs=[rows(D), rows(DN_WIDTH), rows(CONF_WIDTH), full(wo1), full(wo2), full(nw),
                  full(wr_hi), full(wr_lo), full(br)],
        out_specs=[rows(D), pl.BlockSpec((tm * TOKEN_TILE, LANES), lambda i: (i, 0)), rows(LANES),
                   pl.BlockSpec((SUBLANES, tm), lambda i: (0, i)),
                   pl.BlockSpec((1, LANES), lambda i: (0, 0))],
        out_shape=[jax.ShapeDtypeStruct((T, D), F32),
                   jax.ShapeDtypeStruct((T * TOKEN_TILE, LANES), F32),
                   jax.ShapeDtypeStruct((T, LANES), F32), jax.ShapeDtypeStruct((SUBLANES, T), F32),
                   jax.ShapeDtypeStruct((1, LANES), F32)],
        scratch_shapes=[pltpu.VMEM((1, LANES), F32)],
        compiler_params=_params("arbitrary"),
        name="out_router",
    )(x, ydn, ycf, wo1, wo2, nw, wr_hi, wr_lo, br)


def _dispatch_kernel(dest_ref, zb_ref, h_ref, xs_ref, zero_ref, row_sem, zero_sem):
    tm = h_ref.shape[0] // TOKEN_TILE
    base = pl.program_id(0) * tm

    def zero_copy(e):
        start = pl.multiple_of(zb_ref[e] * TOKEN_TILE, SLOT_BLOCK * TOKEN_TILE)
        return pltpu.make_async_copy(zero_ref, xs_ref.at[pl.ds(start, SLOT_BLOCK * TOKEN_TILE)],
                                     zero_sem)

    @pl.when(pl.program_id(0) == 0)
    def _():
        zero_ref[...] = jnp.zeros_like(zero_ref)

        def start(e, carry):
            @pl.when(zb_ref[e] >= 0)
            def _():
                zero_copy(e).start()
            return carry

        def wait(e, carry):
            @pl.when(zb_ref[e] >= 0)
            def _():
                zero_copy(e).wait()
            return carry

        lax.fori_loop(0, 2 * N_EXPERTS, start, 0)
        lax.fori_loop(0, 2 * N_EXPERTS, wait, 0)

    def row_copy(r, k):
        return pltpu.make_async_copy(h_ref.at[_tile_rows(r)],
                                     xs_ref.at[_tile_rows(dest_ref[2 * (base + r) + k])], row_sem)

    def start(r, carry):
        row_copy(r, 0).start(priority=0)
        row_copy(r, 1).start(priority=1)
        return carry

    lax.fori_loop(0, tm, start, 0, unroll=4)
    for _ in range(2):
        pltpu.make_async_copy(h_ref, xs_ref.at[pl.ds(0, tm * TOKEN_TILE)], row_sem).wait()


def _dispatch(dest, zero_blocks, h, n_slots, tm=256):
    T = h.shape[0] // TOKEN_TILE
    return pl.pallas_call(
        _dispatch_kernel,
        grid_spec=pltpu.PrefetchScalarGridSpec(
            num_scalar_prefetch=2,
            grid=(T // tm,),
            in_specs=[pl.BlockSpec((tm * TOKEN_TILE, LANES), lambda i, *_: (i, 0))],
            out_specs=pl.BlockSpec(memory_space=pl.ANY),
            scratch_shapes=[pltpu.VMEM((SLOT_BLOCK * TOKEN_TILE, LANES), F32),
                            pltpu.SemaphoreType.DMA(()), pltpu.SemaphoreType.DMA(())],
        ),
        out_shape=jax.ShapeDtypeStruct((n_slots * TOKEN_TILE, LANES), F32),
        compiler_params=_params("arbitrary"),
        name="dispatch",
    )(dest, zero_blocks, h)


def _experts_kernel(be_ref, nused_ref, xs_ref, wg_ref, wu_ref, wd_ref, ys_ref,
                    wg_s, wu_s, wd_s):
    i = pl.program_id(0)

    @pl.when(i < nused_ref[0])
    def _():
        changed = jnp.logical_or(i == 0, be_ref[i] != be_ref[jnp.maximum(i - 1, 0)])

        @pl.when(changed)
        def _():
            wg_s[...] = wg_ref[0, 0].astype(BF16)
            wu_s[...] = wu_ref[0, 0].astype(BF16)
            wd_s[...] = wd_ref[0, 0].astype(BF16)

        xb = _load_token_tiles(xs_ref, SLOT_BLOCK).astype(BF16)
        act = _silu(_dot(xb, wg_s[...])) * _dot(xb, wu_s[...])
        _store_token_tiles(ys_ref, 0, _dot(act.astype(BF16), wd_s[...]))

    @pl.when(i >= nused_ref[0])
    def _():
        ys_ref[...] = jnp.zeros_like(ys_ref)


def _experts(block_expert, n_used, xs, w_gate, w_up, w_down, layer):
    D, d_expert = w_gate.shape[2:]
    nb = xs.shape[0] // (SLOT_BLOCK * TOKEN_TILE)
    slot_in = lambda i, be, nu: (jnp.minimum(i, nu[0] - 1), 0)
    wsel = lambda i, be, nu: (layer, be[i], 0, 0)
    return pl.pallas_call(
        _experts_kernel,
        grid_spec=pltpu.PrefetchScalarGridSpec(
            num_scalar_prefetch=2,
            grid=(nb,),
            in_specs=[pl.BlockSpec((SLOT_BLOCK * TOKEN_TILE, LANES), slot_in),
                      pl.BlockSpec((1, 1, D, d_expert), wsel),
                      pl.BlockSpec((1, 1, D, d_expert), wsel),
                      pl.BlockSpec((1, 1, d_expert, D), wsel)],
            out_specs=pl.BlockSpec((SLOT_BLOCK * TOKEN_TILE, LANES), lambda i, be, nu: (i, 0)),
            scratch_shapes=[pltpu.VMEM((D, d_expert), BF16), pltpu.VMEM((D, d_expert), BF16),
                            pltpu.VMEM((d_expert, D), BF16)],
        ),
        out_shape=jax.ShapeDtypeStruct(xs.shape, F32),
        compiler_params=_params("arbitrary"),
        name="experts",
    )(block_expert, n_used, xs, w_gate, w_up, w_down)


def _combine_kernel(dest_ref, x_ref, route_ref, nw_ref, ys_ref, o_ref, gbuf_ref, sem, *, final_norm):
    tm = x_ref.shape[0]
    base = pl.program_id(0) * tm

    def row_copy(r, k):
        return pltpu.make_async_copy(ys_ref.at[_tile_rows(dest_ref[2 * (base + r) + k])],
                                     gbuf_ref.at[k, _tile_rows(r)], sem)

    def start(r, carry):
        row_copy(r, 0).start(priority=0)
        row_copy(r, 1).start(priority=1)
        return carry

    lax.fori_loop(0, tm, start, 0)
    for k in range(2):
        pltpu.make_async_copy(ys_ref.at[pl.ds(0, tm * TOKEN_TILE)], gbuf_ref.at[k], sem).wait()
    route = route_ref[...]
    g0 = route[:, ROUTE_GATE0:ROUTE_GATE0 + 1]
    g1 = route[:, ROUTE_GATE1:ROUTE_GATE1 + 1]
    x = x_ref[...] + (_load_token_tiles(gbuf_ref.at[0], tm) * g0 + _load_token_tiles(gbuf_ref.at[1], tm) * g1)
    o_ref[...] = _rmsnorm(x, nw_ref[...]) if final_norm else x


def _combine(dest, x, route, nw, ys, final_norm, tm=256):
    T, D = x.shape
    return pl.pallas_call(
        functools.partial(_combine_kernel, final_norm=final_norm),
        grid_spec=pltpu.PrefetchScalarGridSpec(
            num_scalar_prefetch=1,
            grid=(T // tm,),
            in_specs=[pl.BlockSpec((tm, D), lambda i, d: (i, 0)),
                      pl.BlockSpec((tm, LANES), lambda i, d: (i, 0)),
                      pl.BlockSpec((1, D), lambda i, d: (0, 0)),
                      pl.BlockSpec(memory_space=pl.ANY)],
            out_specs=pl.BlockSpec((tm, D), lambda i, d: (i, 0)),
            scratch_shapes=[pltpu.VMEM((2, tm * TOKEN_TILE, LANES), F32), pltpu.SemaphoreType.DMA(())],
        ),
        out_shape=jax.ShapeDtypeStruct((T, D), F32),
        compiler_params=_params("arbitrary"),
        name="combine",
    )(dest, x, route, nw, ys)


def _slot_plan(plan, counts_row, n_tokens):
    n_blocks = 2 * n_tokens // SLOT_BLOCK + N_EXPERTS
    counts = counts_row[0, :N_EXPERTS].astype(jnp.int32)
    padded = (counts + SLOT_BLOCK - 1) // SLOT_BLOCK * SLOT_BLOCK
    pad_ends = jnp.cumsum(padded)
    pad_starts = pad_ends - padded
    expert = plan[ROUTE_EXPERT0:ROUTE_EXPERT1 + 1].astype(jnp.int32)
    rank = plan[ROUTE_RANK0:ROUTE_RANK1 + 1].astype(jnp.int32)
    hot = expert[..., None] == jnp.arange(N_EXPERTS, dtype=jnp.int32)
    dest = jnp.sum(jnp.where(hot, pad_starts, 0), axis=-1) + rank
    n_used = pad_ends[-1] // SLOT_BLOCK
    first_row = jnp.minimum(jnp.arange(n_blocks, dtype=jnp.int32), n_used - 1) * SLOT_BLOCK
    block_expert = jnp.sum(pad_ends[None, :] <= first_row[:, None], axis=1).astype(jnp.int32)
    block_expert = jnp.minimum(block_expert, N_EXPERTS - 1)
    last_blocks = jnp.where(counts > 0, pad_ends - SLOT_BLOCK, -1)
    tail = n_used + jnp.arange(N_EXPERTS, dtype=jnp.int32)
    tail_blocks = jnp.where(tail < n_blocks, tail * SLOT_BLOCK, -1)
    zero_blocks = jnp.concatenate([last_blocks, tail_blocks]).astype(jnp.int32)
    return (dest.T.reshape(-1), zero_blocks, block_expert, n_used.astype(jnp.int32).reshape(1),
            n_blocks * SLOT_BLOCK)


def _lane_pad(a, lane0, width=LANES):
    return jnp.pad(a.astype(F32), ((0, 0), (lane0, width - lane0 - a.shape[1])))


def kernel(x, norm_mix, w_in, conv_qkv, a_log, dt_bias, o_norm, w_dw, b_dw, ln_g, ln_b,
           w_out, norm_ffn, w_rg, b_rg, w_re, b_re, w_gate, w_up, w_down, norm_final):
    batch, seq, d_model = x.shape
    T = batch * seq
    depth = w_in.shape[0]
    o_z = 3 * DN_WIDTH
    o_b = 4 * DN_WIDTH
    o_c = o_b + 2 * DN_HEADS
    xt = x.reshape(T, d_model)
    for l in range(depth):
        wi = w_in[l]
        qkv, z, ba, glu = _in_proj(
            xt, norm_mix[l][None, :], wi[:, :o_z].astype(BF16), wi[:, o_z:o_b].astype(BF16),
            _lane_pad(wi[:, o_b:o_c], 0).astype(BF16), wi[:, o_c:].astype(BF16))
        y_dn = _deltanet(qkv, z, ba, conv_qkv[l], _lane_pad(a_log[l][None, :], DECAY_LANE0),
                         _lane_pad(dt_bias[l][None, :], DECAY_LANE0), o_norm[l][None, :], batch)
        y_cf = _conformer(glu, w_dw[l], b_dw[l][None, :], ln_g[l][None, :], ln_b[l][None, :], batch)
        wr = jnp.concatenate([w_re[l], _lane_pad(w_rg[l], 0, LANES - N_EXPERTS)], axis=1)
        br = jnp.concatenate([b_re[l][None, :], _lane_pad(b_rg[l][None, :], 0, LANES - N_EXPERTS)], axis=1)
        x2, h, route, plan, counts = _out_router(
            xt, y_dn, y_cf, w_out[l][:DN_WIDTH].astype(BF16), w_out[l][DN_WIDTH:].astype(BF16),
            norm_ffn[l][None, :], wr, br)
        dest, zero_blocks, block_expert, n_used, n_slots = _slot_plan(plan, counts, T)
        xs = _dispatch(dest, zero_blocks, h, n_slots)
        ys = _experts(block_expert, n_used, xs, w_gate, w_up, w_down, l)
        last = l == depth - 1
        xt = _combine(dest, x2, route, norm_final[None, :] if last else norm_ffn[l][None, :],
                      ys, final_norm=last)
    return xt.reshape(batch, seq, d_model)
```

```python
import functools

import jax
import jax.numpy as jnp
from jax import lax
from jax.experimental import pallas as pl
from jax.experimental.pallas import tpu as pltpu

EPS = 1e-6
CHUNK = 64
DN_HEADS = 4
DN_HEAD_DIM = 128
DN_WIDTH = DN_HEADS * DN_HEAD_DIM
SHORT_CONV = 4
CONF_WIDTH = 512
CONF_KERNEL = 31
N_GROUPS = 4
EXPERTS_PER_GROUP = 8
N_EXPERTS = N_GROUPS * EXPERTS_PER_GROUP
D_EXPERT = 512
LANES = 128
SUBLANES = 8
GROUP_LANE0 = N_EXPERTS
BETA_LANE0 = 0
DECAY_LANE0 = DN_HEADS
SLOT_BLOCK = 256
TOKEN_TILE = 8
CHUNKS_PER_ITER = 4
VMEM_LIMIT = 56 * 1024 * 1024

F32 = jnp.float32
BF16 = jnp.bfloat16
HIGHEST = lax.Precision.HIGHEST
NT_DIMS = (((1,), (1,)), ((), ()))
TN_DIMS = (((0,), (0,)), ((), ()))


def _params(*semantics):
    return pltpu.CompilerParams(dimension_semantics=semantics, vmem_limit_bytes=VMEM_LIMIT)


def _sigmoid(x):
    return 1.0 / (1.0 + jnp.exp(-x))


def _silu(x):
    return x * _sigmoid(x)


def _rmsnorm(x, w):
    return x * lax.rsqrt(jnp.mean(x * x, axis=-1, keepdims=True) + EPS) * w


def _dot(a, b, **kw):
    return jnp.dot(a, b, preferred_element_type=F32, **kw)


def _store_token_tiles(ref, row0, x):
    n = x.shape[0]
    for j in range(TOKEN_TILE):
        ref[pl.ds(row0 * TOKEN_TILE + j, n, stride=TOKEN_TILE), :] = x[:, j * LANES:(j + 1) * LANES]


def _load_token_tiles(ref, n):
    return jnp.concatenate([ref[pl.ds(j, n, stride=TOKEN_TILE), :] for j in range(TOKEN_TILE)], axis=1)


def _tile_rows(row):
    return pl.ds(pl.multiple_of(row * TOKEN_TILE, TOKEN_TILE), TOKEN_TILE)


def _in_proj_kernel(x_ref, nw_ref, wqkv_ref, wz_ref, wba_ref, wglu_ref,
                    qkv_ref, z_ref, ba_ref, glu_ref):
    hb = _rmsnorm(x_ref[...], nw_ref[...]).astype(BF16)
    qkv_ref[...] = _dot(hb, wqkv_ref[...])
    z_ref[...] = _dot(hb, wz_ref[...])
    ba_ref[...] = _dot(hb, wba_ref[...])
    glu_ref[...] = _dot(hb, wglu_ref[...])


def _in_proj(x, nw, wqkv, wz, wba, wglu, tm=512):
    T, D = x.shape
    full = lambda a: pl.BlockSpec(a.shape, lambda i: (0, 0), pipeline_mode=pl.Buffered(1))
    rows = lambda n: pl.BlockSpec((tm, n), lambda i: (i, 0))
    widths = (wqkv.shape[1], wz.shape[1], wba.shape[1], wglu.shape[1])
    return pl.pallas_call(
        _in_proj_kernel,
        grid=(T // tm,),
        in_specs=[rows(D), full(nw), full(wqkv), full(wz), full(wba), full(wglu)],
        out_specs=[rows(n) for n in widths],
        out_shape=[jax.ShapeDtypeStruct((T, n), F32) for n in widths],
        compiler_params=_params("parallel"),
        name="in_proj",
    )(x, nw, wqkv, wz, wba, wglu)


def _deltanet_kernel(qkv_ref, z_ref, ba_ref, cw_ref, alog_ref, dtb_ref, onorm_ref, y_ref,
                     state_ref, tail_ref, cbuf_ref, q_s, k_s, v_s, beta_s, gc_s, gct_s, gl_s,
                     u_s, ws_s, vn_s):
    tb = qkv_ref.shape[0]
    n_chunks = tb // CHUNK
    halo = SUBLANES

    @pl.when(pl.program_id(1) == 0)
    def _():
        state_ref[...] = jnp.zeros_like(state_ref)
        tail_ref[...] = jnp.zeros_like(tail_ref)

    cbuf_ref[0:halo, :] = tail_ref[...]
    cbuf_ref[halo:halo + tb, :] = qkv_ref[...]
    tail_ref[...] = qkv_ref[tb - halo:tb, :]
    first = halo - (SHORT_CONV - 1)
    acc = cw_ref[0:1, :] * cbuf_ref[first:first + tb, :]
    for j in range(1, SHORT_CONV):
        acc = acc + cw_ref[j:j + 1, :] * cbuf_ref[first + j:first + j + tb, :]
    c = _silu(acc)

    for h in range(DN_HEADS):
        hs = slice(h * DN_HEAD_DIM, (h + 1) * DN_HEAD_DIM)
        q = c[:, h * DN_HEAD_DIM:(h + 1) * DN_HEAD_DIM]
        k = c[:, DN_WIDTH + h * DN_HEAD_DIM:DN_WIDTH + (h + 1) * DN_HEAD_DIM]
        q_s[:, hs] = q * lax.rsqrt(jnp.sum(q * q, axis=-1, keepdims=True) + EPS) * (DN_HEAD_DIM ** -0.5)
        k_s[:, hs] = k * lax.rsqrt(jnp.sum(k * k, axis=-1, keepdims=True) + EPS)
    v_s[...] = c[:, 2 * DN_WIDTH:3 * DN_WIDTH]

    ba = ba_ref[...]
    beta_s[...] = _sigmoid(ba)
    xa = ba + dtb_ref[...]
    softplus = jnp.maximum(xa, 0.0) + jnp.log1p(jnp.exp(-jnp.abs(xa)))
    g = -jnp.exp(alog_ref[...]) * softplus

    pair = 2 * DN_HEAD_DIM
    row = lax.broadcasted_iota(jnp.int32, (CHUNK, LANES), 0)
    lane = lax.broadcasted_iota(jnp.int32, (CHUNK, LANES), 1)
    second = lane >= CHUNK
    col = jnp.where(second, lane - CHUNK, lane)
    incl = row >= col
    strict = row > col
    eye = (row == col).astype(F32)
    trow = lax.broadcasted_iota(jnp.int32, (CHUNK, CHUNK), 0)
    tcol = lax.broadcasted_iota(jnp.int32, (CHUNK, CHUNK), 1)
    tril = (trow >= tcol).astype(F32)
    sel = (lax.broadcasted_iota(jnp.int32, (SUBLANES, LANES), 0)
           == lax.broadcasted_iota(jnp.int32, (SUBLANES, LANES), 1)).astype(F32)

    for ci in range(n_chunks):
        gc = _dot(tril, g[ci * CHUNK:(ci + 1) * CHUNK, :], precision=HIGHEST)
        gc_s[ci * CHUNK:(ci + 1) * CHUNK, :] = gc
        nxt = pltpu.roll(gc, LANES - 1, 1)
        gct_s[ci] = lax.dot_general(sel, jnp.concatenate([gc, nxt], axis=0), NT_DIMS,
                                    precision=HIGHEST, preferred_element_type=F32)
        gl_s[ci:ci + 1, :] = jnp.exp(gc[CHUNK - 1:CHUNK, :])

    def block_diag(m):
        half = m.shape[1] // 2
        z = jnp.zeros_like(m[:, :half])
        return jnp.concatenate([jnp.concatenate([m[:, :half], z], axis=1),
                                jnp.concatenate([z, m[:, half:]], axis=1)], axis=0)

    def block_diag_lanes(m):
        return jnp.concatenate([jnp.where(second, 0.0, m), jnp.where(second, m, 0.0)], axis=0)

    def intra_chunk(it, carry):
        streams = [(it * CHUNKS_PER_ITER + sub, p) for sub in range(CHUNKS_PER_ITER)
                   for p in range(DN_HEADS // 2)]
        st = []
        for ci, p in streams:
            rows = pl.ds(pl.multiple_of(ci * CHUNK, CHUNK), CHUNK)
            ps = slice(p * pair, (p + 1) * pair)
            gc = gc_s[rows, :]
            l0 = DECAY_LANE0 + 2 * p
            gcol0, gcol1 = gc[:, l0:l0 + 1], gc[:, l0 + 1:l0 + 2]
            gcol = jnp.where(second, gcol1, gcol0)
            grow = gct_s[ci][l0:l0 + 1, :]
            decay = jnp.where(incl, jnp.exp(jnp.where(incl, gcol - grow, 0.0)), 0.0)
            q, k, v = q_s[rows, ps], k_s[rows, ps], v_s[rows, ps]
            b0 = beta_s[rows, BETA_LANE0 + 2 * p:BETA_LANE0 + 2 * p + 1]
            b1 = beta_s[rows, BETA_LANE0 + 2 * p + 1:BETA_LANE0 + 2 * p + 2]
            kkqk = lax.dot_general(jnp.concatenate([k, q], axis=0).astype(BF16),
                                   block_diag(k.astype(BF16)), NT_DIMS, preferred_element_type=F32)
            st.append(dict(ci=ci, p=p, rows=rows, ps=ps, gc=gc, l0=l0, gcol0=gcol0, gcol1=gcol1,
                           decay=decay, q=q, k=k, v=v, b0=b0, b1=b1, kkqk=kkqk))
        for s in st:
            a = jnp.where(strict, jnp.where(second, s['b1'], s['b0']) * s['kkqk'][:CHUNK] * s['decay'], 0.0)
            s['inv'] = eye - a
            s['pw'] = _dot(a.astype(BF16), block_diag_lanes(a).astype(BF16))
        for _ in range(4):
            for s in st:
                r = _dot(jnp.concatenate([s['inv'], s['pw']], axis=0).astype(BF16),
                         block_diag_lanes(s['pw']).astype(BF16))
                s['inv'] = s['inv'] + r[:CHUNK]
                s['pw'] = r[CHUNK:]
        for s in st:
            s['inv'] = s['inv'] + _dot(s['inv'].astype(BF16), block_diag_lanes(s['pw']).astype(BF16))
        for s in st:
            eg0, eg1 = jnp.exp(s['gcol0']), jnp.exp(s['gcol1'])
            k, v, q = s['k'], s['v'], s['q']
            hd = DN_HEAD_DIM
            rhs = jnp.concatenate([s['b0'] * v[:, :hd], (s['b0'] * eg0) * k[:, :hd],
                                   s['b1'] * v[:, hd:], (s['b1'] * eg1) * k[:, hd:]], axis=1)
            sol = _dot(s['inv'].astype(BF16), block_diag(rhs.astype(BF16)))
            ci, p = s['ci'], s['p']
            u_s[s['rows'], s['ps']] = jnp.concatenate([sol[:, :hd], sol[:, 2 * hd:3 * hd]], axis=1)
            ws_s[ci, p, 0:CHUNK, :] = jnp.concatenate(
                [sol[:, hd:2 * hd], sol[:, 3 * hd:]], axis=1).astype(BF16)
            ws_s[ci, p, CHUNK:2 * CHUNK, :] = jnp.concatenate(
                [q[:, :hd] * eg0, q[:, hd:] * eg1], axis=1).astype(BF16)
            vn_s[ci, p, 0:CHUNK, :] = (s['kkqk'][CHUNK:] * s['decay']).astype(BF16)
            gc, l0 = s['gc'], s['l0']
            glast0 = gc[CHUNK - 1:CHUNK, l0:l0 + 1]
            glast1 = gc[CHUNK - 1:CHUNK, l0 + 1:l0 + 2]
            kdec = jnp.concatenate([k[:, :hd] * jnp.exp(glast0 - s['gcol0']),
                                    k[:, hd:] * jnp.exp(glast1 - s['gcol1'])], axis=0)
            vn_s[ci, p, CHUNK:CHUNK + hd, :] = jnp.transpose(kdec).astype(BF16)
        return carry

    lax.fori_loop(0, n_chunks // CHUNKS_PER_ITER, intra_chunk, 0)

    def inter_chunk(ci, carry):
        rows = pl.ds(pl.multiple_of(ci * CHUNK, CHUNK), CHUNK)
        glrow = gl_s[pl.ds(ci, 1), :]
        hd = DN_HEAD_DIM
        pairs = range(DN_HEADS // 2)
        r1 = [_dot(ws_s[ci, p], block_diag(jnp.concatenate(
            [state_ref[2 * p], state_ref[2 * p + 1]], axis=1).astype(BF16))) for p in pairs]
        v_new = [u_s[rows, p * pair:(p + 1) * pair] - r1[p][:CHUNK] for p in pairs]
        r2 = [_dot(vn_s[ci, p], block_diag(v_new[p].astype(BF16))) for p in pairs]
        for p in pairs:
            o2 = r1[p][CHUNK:] + r2[p][:CHUNK]
            for j in range(2):
                h = 2 * p + j
                hs = slice(h * hd, (h + 1) * hd)
                lane_h = DECAY_LANE0 + h
                state_ref[h] = (state_ref[h] * glrow[:, lane_h:lane_h + 1]
                                + r2[p][CHUNK:, j * hd:(j + 1) * hd])
                o = o2[:, j * hd:(j + 1) * hd]
                o = o * lax.rsqrt(jnp.mean(o * o, axis=-1, keepdims=True) + EPS) * onorm_ref[...]
                y_ref[rows, hs] = (o * _silu(z_ref[rows, hs])).astype(y_ref.dtype)
        return carry

    lax.fori_loop(0, n_chunks, inter_chunk, 0, unroll=2)


def _deltanet(qkv, z, ba, cw, alog, dtb, onorm, batch, tb=512):
    T = qkv.shape[0]
    nb = T // batch // tb
    n_chunks = tb // CHUNK
    full = lambda a: pl.BlockSpec(a.shape, lambda b, i: (0, 0))
    rows = lambda n: pl.BlockSpec((tb, n), lambda b, i: (b * nb + i, 0))
    scr = lambda *s: pltpu.VMEM(s, F32)
    return pl.pallas_call(
        _deltanet_kernel,
        grid=(batch, nb),
        in_specs=[rows(3 * DN_WIDTH), rows(DN_WIDTH), rows(LANES),
                  full(cw), full(alog), full(dtb), full(onorm)],
        out_specs=rows(DN_WIDTH),
        out_shape=jax.ShapeDtypeStruct((T, DN_WIDTH), BF16),
        scratch_shapes=[
            scr(DN_HEADS, DN_HEAD_DIM, DN_HEAD_DIM),
            scr(SUBLANES, 3 * DN_WIDTH),
            scr(SUBLANES + tb, 3 * DN_WIDTH),
            scr(tb, DN_WIDTH), scr(tb, DN_WIDTH), scr(tb, DN_WIDTH),
            scr(tb, LANES),
            scr(tb, LANES), scr(n_chunks, SUBLANES, 2 * CHUNK), scr(n_chunks, LANES),
            scr(tb, DN_WIDTH),
            pltpu.VMEM((n_chunks, DN_HEADS // 2, 2 * CHUNK, 2 * DN_HEAD_DIM), BF16),
            pltpu.VMEM((n_chunks, DN_HEADS // 2, CHUNK + DN_HEAD_DIM, 2 * CHUNK), BF16),
        ],
        compiler_params=_params("parallel", "arbitrary"),
        name="deltanet",
    )(qkv, z, ba, cw, alog, dtb, onorm)


def _conformer_kernel(glu_ref, wdw_ref, bdw_ref, lng_ref, lnb_ref, y_ref, cbuf_ref):
    tc = glu_ref.shape[0]
    halo = 4 * SUBLANES

    @pl.when(pl.program_id(1) == 0)
    def _():
        cbuf_ref[0:halo, :] = jnp.zeros((halo, CONF_WIDTH), F32)

    u = glu_ref[...]
    g = u[:, :CONF_WIDTH] * _sigmoid(u[:, CONF_WIDTH:])
    cbuf_ref[halo:halo + tc, :] = g
    first = halo - (CONF_KERNEL - 1)
    acc = bdw_ref[...]
    for phase in range(SUBLANES):
        taps = [j for j in range(CONF_KERNEL) if (first + j) % SUBLANES == phase]
        n = tc if phase == 0 else tc + SUBLANES
        part = None
        for j in taps:
            r0 = first + j - phase
            term = wdw_ref[j:j + 1, :] * cbuf_ref[r0:r0 + n, :]
            part = term if part is None else part + term
        acc = acc + part[phase:phase + tc, :]
    cbuf_ref[0:halo, :] = cbuf_ref[tc:tc + halo, :]
    mu = jnp.mean(acc, axis=-1, keepdims=True)
    d = acc - mu
    var = jnp.mean(d * d, axis=-1, keepdims=True)
    y = d * lax.rsqrt(var + EPS) * lng_ref[...] + lnb_ref[...]
    y_ref[...] = _silu(y).astype(y_ref.dtype)


def _conformer(glu, wdw, bdw, lng, lnb, batch, tc=512):
    T = glu.shape[0]
    nb = T // batch // tc
    full = lambda a: pl.BlockSpec(a.shape, lambda b, i: (0, 0))
    rows = lambda n: pl.BlockSpec((tc, n), lambda b, i: (b * nb + i, 0))
    return pl.pallas_call(
        _conformer_kernel,
        grid=(batch, nb),
        in_specs=[rows(2 * CONF_WIDTH), full(wdw), full(bdw), full(lng), full(lnb)],
        out_specs=rows(CONF_WIDTH),
        out_shape=jax.ShapeDtypeStruct((T, CONF_WIDTH), BF16),
        scratch_shapes=[pltpu.VMEM((4 * SUBLANES + tc, CONF_WIDTH), F32)],
        compiler_params=_params("parallel", "arbitrary"),
        name="conformer",
    )(glu, wdw, bdw, lng, lnb)


ROUTE_GATE0, ROUTE_GATE1, ROUTE_EXPERT0, ROUTE_EXPERT1, ROUTE_RANK0, ROUTE_RANK1 = range(6)


def _lane_argmax(vals, valid, lane):
    neg = jnp.float32(-1e30)
    m = jnp.max(jnp.where(valid, vals, neg), axis=-1, keepdims=True)
    idx = jnp.min(jnp.where(valid & (vals == m), lane, LANES), axis=-1, keepdims=True)
    return m, idx


def _out_router_kernel(x_ref, ydn_ref, ycf_ref, wo1_ref, wo2_ref, nw_ref, wrh_ref, wrl_ref, br_ref,
                       x2_ref, h_ref, route_ref, plan_ref, cnt_ref, base_ref):
    tm = x_ref.shape[0]

    @pl.when(pl.program_id(0) == 0)
    def _():
        base_ref[...] = jnp.zeros_like(base_ref)

    x2 = x_ref[...] + _dot(ydn_ref[...], wo1_ref[...]) + _dot(ycf_ref[...], wo2_ref[...])
    x2_ref[...] = x2
    h = _rmsnorm(x2, nw_ref[...])
    _store_token_tiles(h_ref, 0, h)
    h_hi = h.astype(BF16)
    h_lo = (h - h_hi.astype(F32)).astype(BF16)
    hh = _dot(jnp.concatenate([h_hi, h_lo], axis=0), wrh_ref[...])
    logits = hh[:tm] + hh[tm:] + _dot(h_hi, wrl_ref[...]) + br_ref[...]

    lane = lax.broadcasted_iota(jnp.int32, (tm, LANES), 1)
    gvalid = (lane >= GROUP_LANE0) & (lane < GROUP_LANE0 + N_GROUPS)
    gmax, _ = _lane_argmax(logits, gvalid, lane)
    gexp = jnp.where(gvalid, jnp.exp(jnp.where(gvalid, logits - gmax, 0.0)), 0.0)
    p_grp = gexp / jnp.sum(gexp, axis=-1, keepdims=True)
    p_g, g_lane = _lane_argmax(p_grp, gvalid, lane)
    g_sel = g_lane - GROUP_LANE0
    evalid = (lane < N_EXPERTS) & ((lane // EXPERTS_PER_GROUP) == g_sel)
    emax, _ = _lane_argmax(logits, evalid, lane)
    eexp = jnp.where(evalid, jnp.exp(jnp.where(evalid, logits - emax, 0.0)), 0.0)
    p_e = eexp / jnp.sum(eexp, axis=-1, keepdims=True)
    p1, e1 = _lane_argmax(p_e, evalid, lane)
    p2, e2 = _lane_argmax(p_e, evalid & (lane != e1), lane)
    psum = p1 + p2
    gate1 = p_g * p1 / psum
    gate2 = p_g * p2 / psum

    hot1 = (lane == e1).astype(F32)
    hot2 = (lane == e2).astype(F32)
    hot = hot1 + hot2
    trow = lax.broadcasted_iota(jnp.int32, (tm, tm), 0)
    tcol = lax.broadcasted_iota(jnp.int32, (tm, tm), 1)
    before = (trow > tcol).astype(BF16)
    seen = base_ref[...] + _dot(before, hot.astype(BF16))
    rank1 = jnp.sum(hot1 * seen, axis=-1, keepdims=True)
    rank2 = jnp.sum(hot2 * seen, axis=-1, keepdims=True)
    base_ref[...] = base_ref[...] + jnp.sum(hot, axis=0, keepdims=True)
    cnt_ref[...] = base_ref[...]

    route = jnp.zeros((tm, LANES), F32)
    for pos, val in ((ROUTE_GATE0, gate1), (ROUTE_GATE1, gate2),
                     (ROUTE_EXPERT0, e1.astype(F32)), (ROUTE_EXPERT1, e2.astype(F32)),
                     (ROUTE_RANK0, rank1), (ROUTE_RANK1, rank2)):
        route = jnp.where(lane == pos, val, route)
    route_ref[...] = route
    plan_ref[...] = jnp.transpose(route)[:SUBLANES, :]


def _out_router(x, ydn, ycf, wo1, wo2, nw, wr, br, tm=512):
    T, D = x.shape
    full = lambda a: pl.BlockSpec(a.shape, lambda i: (0, 0), pipeline_mode=pl.Buffered(1))
    rows = lambda n: pl.BlockSpec((tm, n), lambda i: (i, 0))
    wr_hi = wr.astype(BF16)
    wr_lo = (wr - wr_hi.astype(F32)).astype(BF16)
    return pl.pallas_call(
        _out_router_kernel,
        grid=(T // tm,),
        in_specs=[rows(D), rows(DN_WIDTH), rows(CONF_WIDTH), full(wo1), full(wo2), full(nw),
                  full(wr_hi), full(wr_lo), full(br)],
        out_specs=[rows(D), pl.BlockSpec((tm * TOKEN_TILE, LANES), lambda i: (i, 0)), rows(LANES),
                   pl.BlockSpec((SUBLANES, tm), lambda i: (0, i)),
                   pl.BlockSpec((1, LANES), lambda i: (0, 0))],
        out_shape=[jax.ShapeDtypeStruct((T, D), F32),
                   jax.ShapeDtypeStruct((T * TOKEN_TILE, LANES), F32),
                   jax.ShapeDtypeStruct((T, LANES), F32), jax.ShapeDtypeStruct((SUBLANES, T), F32),
                   jax.ShapeDtypeStruct((1, LANES), F32)],
        scratch_shapes=[pltpu.VMEM((1, LANES), F32)],
        compiler_params=_params("arbitrary"),
        name="out_router",
    )(x, ydn, ycf, wo1, wo2, nw, wr_hi, wr_lo, br)


def _dispatch_kernel(dest_ref, zb_ref, h_ref, xs_ref, zero_ref, row_sem, zero_sem):
    tm = h_ref.shape[0] // TOKEN_TILE
    base = pl.program_id(0) * tm

    def zero_copy(e):
        start = pl.multiple_of(zb_ref[e] * TOKEN_TILE, SLOT_BLOCK * TOKEN_TILE)
        return pltpu.make_async_copy(zero_ref, xs_ref.at[pl.ds(start, SLOT_BLOCK * TOKEN_TILE)],
                                     zero_sem)

    @pl.when(pl.program_id(0) == 0)
    def _():
        zero_ref[...] = jnp.zeros_like(zero_ref)

        def start(e, carry):
            @pl.when(zb_ref[e] >= 0)
            def _():
                zero_copy(e).start()
            return carry

        def wait(e, carry):
            @pl.when(zb_ref[e] >= 0)
            def _():
                zero_copy(e).wait()
            return carry

        lax.fori_loop(0, 2 * N_EXPERTS, start, 0)
        lax.fori_loop(0, 2 * N_EXPERTS, wait, 0)

    def row_copy(r, k):
        return pltpu.make_async_copy(h_ref.at[_tile_rows(r)],
                                     xs_ref.at[_tile_rows(dest_ref[2 * (base + r) + k])], row_sem)

    def start(r, carry):
        row_copy(r, 0).start(priority=0)
        row_copy(r, 1).start(priority=1)
        return carry

    lax.fori_loop(0, tm, start, 0, unroll=4)
    for _ in range(2):
        pltpu.make_async_copy(h_ref, xs_ref.at[pl.ds(0, tm * TOKEN_TILE)], row_sem).wait()


def _dispatch(dest, zero_blocks, h, n_slots, tm=256):
    T = h.shape[0] // TOKEN_TILE
    return pl.pallas_call(
        _dispatch_kernel,
        grid_spec=pltpu.PrefetchScalarGridSpec(
            num_scalar_prefetch=2,
            grid=(T // tm,),
            in_specs=[pl.BlockSpec((tm * TOKEN_TILE, LANES), lambda i, *_: (i, 0))],
            out_specs=pl.BlockSpec(memory_space=pl.ANY),
            scratch_shapes=[pltpu.VMEM((SLOT_BLOCK * TOKEN_TILE, LANES), F32),
                            pltpu.SemaphoreType.DMA(()), pltpu.SemaphoreType.DMA(())],
        ),
        out_shape=jax.ShapeDtypeStruct((n_slots * TOKEN_TILE, LANES), F32),
        compiler_params=_params("arbitrary"),
        name="dispatch",
    )(dest, zero_blocks, h)


def _experts_kernel(be_ref, nused_ref, xs_ref, wg_ref, wu_ref, wd_ref, ys_ref,
                    wg_s, wu_s, wd_s):
    i = pl.program_id(0)

    @pl.when(i < nused_ref[0])
    def _():
        changed = jnp.logical_or(i == 0, be_ref[i] != be_ref[jnp.maximum(i - 1, 0)])

        @pl.when(changed)
        def _():
            wg_s[...] = wg_ref[0, 0].astype(BF16)
            wu_s[...] = wu_ref[0, 0].astype(BF16)
            wd_s[...] = wd_ref[0, 0].astype(BF16)

        xb = _load_token_tiles(xs_ref, SLOT_BLOCK).astype(BF16)
        act = _silu(_dot(xb, wg_s[...])) * _dot(xb, wu_s[...])
        _store_token_tiles(ys_ref, 0, _dot(act.astype(BF16), wd_s[...]))

    @pl.when(i >= nused_ref[0])
    def _():
        ys_ref[...] = jnp.zeros_like(ys_ref)


def _experts(block_expert, n_used, xs, w_gate, w_up, w_down, layer):
    D, d_expert = w_gate.shape[2:]
    nb = xs.shape[0] // (SLOT_BLOCK * TOKEN_TILE)
    slot_in = lambda i, be, nu: (jnp.minimum(i, nu[0] - 1), 0)
    wsel = lambda i, be, nu: (layer, be[i], 0, 0)
    return pl.pallas_call(
        _experts_kernel,
        grid_spec=pltpu.PrefetchScalarGridSpec(
            num_scalar_prefetch=2,
            grid=(nb,),
            in_specs=[pl.BlockSpec((SLOT_BLOCK * TOKEN_TILE, LANES), slot_in),
                      pl.BlockSpec((1, 1, D, d_expert), wsel),
                      pl.BlockSpec((1, 1, D, d_expert), wsel),
                      pl.BlockSpec((1, 1, d_expert, D), wsel)],
            out_specs=pl.BlockSpec((SLOT_BLOCK * TOKEN_TILE, LANES), lambda i, be, nu: (i, 0)),
            scratch_shapes=[pltpu.VMEM((D, d_expert), BF16), pltpu.VMEM((D, d_expert), BF16),
                            pltpu.VMEM((d_expert, D), BF16)],
        ),
        out_shape=jax.ShapeDtypeStruct(xs.shape, F32),
        compiler_params=_params("arbitrary"),
        name="experts",
    )(block_expert, n_used, xs, w_gate, w_up, w_down)


def _combine_kernel(dest_ref, x_ref, route_ref, nw_ref, ys_ref, o_ref, gbuf_ref, sem, *, final_norm):
    tm = x_ref.shape[0]
    base = pl.program_id(0) * tm

    def row_copy(r, k):
        return pltpu.make_async_copy(ys_ref.at[_tile_rows(dest_ref[2 * (base + r) + k])],
                                     gbuf_ref.at[k, _tile_rows(r)], sem)

    def start(r, carry):
        row_copy(r, 0).start(priority=0)
        row_copy(r, 1).start(priority=1)
        return carry

    lax.fori_loop(0, tm, start, 0, unroll=4)
    for k in range(2):
        pltpu.make_async_copy(ys_ref.at[pl.ds(0, tm * TOKEN_TILE)], gbuf_ref.at[k], sem).wait()
    route = route_ref[...]
    g0 = route[:, ROUTE_GATE0:ROUTE_GATE0 + 1]
    g1 = route[:, ROUTE_GATE1:ROUTE_GATE1 + 1]
    x = x_ref[...] + (_load_token_tiles(gbuf_ref.at[0], tm) * g0 + _load_token_tiles(gbuf_ref.at[1], tm) * g1)
    o_ref[...] = _rmsnorm(x, nw_ref[...]) if final_norm else x


def _combine(dest, x, route, nw, ys, final_norm, tm=256):
    T, D = x.shape
    return pl.pallas_call(
        functools.partial(_combine_kernel, final_norm=final_norm),
        grid_spec=pltpu.PrefetchScalarGridSpec(
            num_scalar_prefetch=1,
            grid=(T // tm,),
            in_specs=[pl.BlockSpec((tm, D), lambda i, d: (i, 0)),
                      pl.BlockSpec((tm, LANES), lambda i, d: (i, 0)),
                      pl.BlockSpec((1, D), lambda i, d: (0, 0)),
                      pl.BlockSpec(memory_space=pl.ANY)],
            out_specs=pl.BlockSpec((tm, D), lambda i, d: (i, 0)),
            scratch_shapes=[pltpu.VMEM((2, tm * TOKEN_TILE, LANES), F32), pltpu.SemaphoreType.DMA(())],
        ),
        out_shape=jax.ShapeDtypeStruct((T, D), F32),
        compiler_params=_params("arbitrary"),
        name="combine",
    )(dest, x, route, nw, ys)


def _slot_plan(plan, counts_row, n_tokens):
    n_blocks = 2 * n_tokens // SLOT_BLOCK + N_EXPERTS
    counts = counts_row[0, :N_EXPERTS].astype(jnp.int32)
    padded = (counts + SLOT_BLOCK - 1) // SLOT_BLOCK * SLOT_BLOCK
    pad_ends = jnp.cumsum(padded)
    pad_starts = pad_ends - padded
    expert = plan[ROUTE_EXPERT0:ROUTE_EXPERT1 + 1].astype(jnp.int32)
    rank = plan[ROUTE_RANK0:ROUTE_RANK1 + 1].astype(jnp.int32)
    hot = expert[..., None] == jnp.arange(N_EXPERTS, dtype=jnp.int32)
    dest = jnp.sum(jnp.where(hot, pad_starts, 0), axis=-1) + rank
    n_used = pad_ends[-1] // SLOT_BLOCK
    first_row = jnp.minimum(jnp.arange(n_blocks, dtype=jnp.int32), n_used - 1) * SLOT_BLOCK
    block_expert = jnp.sum(pad_ends[None, :] <= first_row[:, None], axis=1).astype(jnp.int32)
    block_expert = jnp.minimum(block_expert, N_EXPERTS - 1)
    last_blocks = jnp.where(counts > 0, pad_ends - SLOT_BLOCK, -1)
    tail = n_used + jnp.arange(N_EXPERTS, dtype=jnp.int32)
    tail_blocks = jnp.where(tail < n_blocks, tail * SLOT_BLOCK, -1)
    zero_blocks = jnp.concatenate([last_blocks, tail_blocks]).astype(jnp.int32)
    return (dest.T.reshape(-1), zero_blocks, block_expert, n_used.astype(jnp.int32).reshape(1),
            n_blocks * SLOT_BLOCK)


def _lane_pad(a, lane0, width=LANES):
    return jnp.pad(a.astype(F32), ((0, 0), (lane0, width - lane0 - a.shape[1])))


def kernel(x, norm_mix, w_in, conv_qkv, a_log, dt_bias, o_norm, w_dw, b_dw, ln_g, ln_b,
           w_out, norm_ffn, w_rg, b_rg, w_re, b_re, w_gate, w_up, w_down, norm_final):
    batch, seq, d_model = x.shape
    T = batch * seq
    depth = w_in.shape[0]
    o_z = 3 * DN_WIDTH
    o_b = 4 * DN_WIDTH
    o_c = o_b + 2 * DN_HEADS
    xt = x.reshape(T, d_model)
    for l in range(depth):
        wi = w_in[l]
        qkv, z, ba, glu = _in_proj(
            xt, norm_mix[l][None, :], wi[:, :o_z].astype(BF16), wi[:, o_z:o_b].astype(BF16),
            _lane_pad(wi[:, o_b:o_c], 0).astype(BF16), wi[:, o_c:].astype(BF16))
        y_dn = _deltanet(qkv, z, ba, conv_qkv[l], _lane_pad(a_log[l][None, :], DECAY_LANE0),
                         _lane_pad(dt_bias[l][None, :], DECAY_LANE0), o_norm[l][None, :], batch)
        y_cf = _conformer(glu, w_dw[l], b_dw[l][None, :], ln_g[l][None, :], ln_b[l][None, :], batch)
        wr = jnp.concatenate([w_re[l], _lane_pad(w_rg[l], 0, LANES - N_EXPERTS)], axis=1)
        br = jnp.concatenate([b_re[l][None, :], _lane_pad(b_rg[l][None, :], 0, LANES - N_EXPERTS)], axis=1)
        x2, h, route, plan, counts = _out_router(
            xt, y_dn, y_cf, w_out[l][:DN_WIDTH].astype(BF16), w_out[l][DN_WIDTH:].astype(BF16),
            norm_ffn[l][None, :], wr, br)
        dest, zero_blocks, block_expert, n_used, n_slots = _slot_plan(plan, counts, T)
        xs = _dispatch(dest, zero_blocks, h, n_slots)
        ys = _experts(block_expert, n_used, xs, w_gate, w_up, w_down, l)
        last = l == depth - 1
        xt = _combine(dest, x2, route, norm_final[None, :] if last else norm_ffn[l][None, :],
                      ys, final_norm=last)
    return xt.reshape(batch, seq, d_model)
```

```python
import functools

import jax
import jax.numpy as jnp
from jax import lax
from jax.experimental import pallas as pl
from jax.experimental.pallas import tpu as pltpu

EPS = 1e-6
CHUNK = 64
DN_HEADS = 4
DN_HEAD_DIM = 128
DN_WIDTH = DN_HEADS * DN_HEAD_DIM
SHORT_CONV = 4
CONF_WIDTH = 512
CONF_KERNEL = 31
N_GROUPS = 4
EXPERTS_PER_GROUP = 8
N_EXPERTS = N_GROUPS * EXPERTS_PER_GROUP
D_EXPERT = 512
LANES = 128
SUBLANES = 8
GROUP_LANE0 = N_EXPERTS
BETA_LANE0 = 0
DECAY_LANE0 = DN_HEADS
SLOT_BLOCK = 512
TOKEN_TILE = 8
CHUNKS_PER_ITER = 4
VMEM_LIMIT = 56 * 1024 * 1024

F32 = jnp.float32
BF16 = jnp.bfloat16
HIGHEST = lax.Precision.HIGHEST
NT_DIMS = (((1,), (1,)), ((), ()))
TN_DIMS = (((0,), (0,)), ((), ()))


def _params(*semantics):
    return pltpu.CompilerParams(dimension_semantics=semantics, vmem_limit_bytes=VMEM_LIMIT)


def _sigmoid(x):
    return 1.0 / (1.0 + jnp.exp(-x))


def _silu(x):
    return x * _sigmoid(x)


def _rmsnorm(x, w):
    return x * lax.rsqrt(jnp.mean(x * x, axis=-1, keepdims=True) + EPS) * w


def _dot(a, b, **kw):
    return jnp.dot(a, b, preferred_element_type=F32, **kw)


def _store_token_tiles(ref, row0, x):
    n = x.shape[0]
    for j in range(TOKEN_TILE):
        ref[pl.ds(row0 * TOKEN_TILE + j, n, stride=TOKEN_TILE), :] = x[:, j * LANES:(j + 1) * LANES]


def _load_token_tiles(ref, n):
    return jnp.concatenate([ref[pl.ds(j, n, stride=TOKEN_TILE), :] for j in range(TOKEN_TILE)], axis=1)


def _tile_rows(row):
    return pl.ds(pl.multiple_of(row * TOKEN_TILE, TOKEN_TILE), TOKEN_TILE)


def _in_proj_kernel(x_ref, nw_ref, wqkv_ref, wz_ref, wba_ref, wglu_ref,
                    qkv_ref, z_ref, ba_ref, glu_ref):
    hb = _rmsnorm(x_ref[...], nw_ref[...]).astype(BF16)
    qkv_ref[...] = _dot(hb, wqkv_ref[...])
    z_ref[...] = _dot(hb, wz_ref[...])
    ba_ref[...] = _dot(hb, wba_ref[...])
    glu_ref[...] = _dot(hb, wglu_ref[...])


def _in_proj(x, nw, wqkv, wz, wba, wglu, tm=512):
    T, D = x.shape
    full = lambda a: pl.BlockSpec(a.shape, lambda i: (0, 0), pipeline_mode=pl.Buffered(1))
    rows = lambda n: pl.BlockSpec((tm, n), lambda i: (i, 0))
    widths = (wqkv.shape[1], wz.shape[1], wba.shape[1], wglu.shape[1])
    return pl.pallas_call(
        _in_proj_kernel,
        grid=(T // tm,),
        in_specs=[rows(D), full(nw), full(wqkv), full(wz), full(wba), full(wglu)],
        out_specs=[rows(n) for n in widths],
        out_shape=[jax.ShapeDtypeStruct((T, n), F32) for n in widths],
        compiler_params=_params("parallel"),
        name="in_proj",
    )(x, nw, wqkv, wz, wba, wglu)


def _deltanet_kernel(qkv_ref, z_ref, ba_ref, cw_ref, alog_ref, dtb_ref, onorm_ref, y_ref,
                     state_ref, tail_ref, cbuf_ref, q_s, k_s, v_s, beta_s, gc_s, gct_s, gl_s,
                     u_s, ws_s, vn_s):
    tb = qkv_ref.shape[0]
    n_chunks = tb // CHUNK
    halo = SUBLANES

    @pl.when(pl.program_id(1) == 0)
    def _():
        state_ref[...] = jnp.zeros_like(state_ref)
        tail_ref[...] = jnp.zeros_like(tail_ref)

    cbuf_ref[0:halo, :] = tail_ref[...]
    cbuf_ref[halo:halo + tb, :] = qkv_ref[...]
    tail_ref[...] = qkv_ref[tb - halo:tb, :]
    first = halo - (SHORT_CONV - 1)
    acc = cw_ref[0:1, :] * cbuf_ref[first:first + tb, :]
    for j in range(1, SHORT_CONV):
        acc = acc + cw_ref[j:j + 1, :] * cbuf_ref[first + j:first + j + tb, :]
    c = _silu(acc)

    for h in range(DN_HEADS):
        hs = slice(h * DN_HEAD_DIM, (h + 1) * DN_HEAD_DIM)
        q = c[:, h * DN_HEAD_DIM:(h + 1) * DN_HEAD_DIM]
        k = c[:, DN_WIDTH + h * DN_HEAD_DIM:DN_WIDTH + (h + 1) * DN_HEAD_DIM]
        q_s[:, hs] = q * lax.rsqrt(jnp.sum(q * q, axis=-1, keepdims=True) + EPS) * (DN_HEAD_DIM ** -0.5)
        k_s[:, hs] = k * lax.rsqrt(jnp.sum(k * k, axis=-1, keepdims=True) + EPS)
    v_s[...] = c[:, 2 * DN_WIDTH:3 * DN_WIDTH]

    ba = ba_ref[...]
    beta_s[...] = _sigmoid(ba)
    xa = ba + dtb_ref[...]
    softplus = jnp.maximum(xa, 0.0) + jnp.log1p(jnp.exp(-jnp.abs(xa)))
    g = -jnp.exp(alog_ref[...]) * softplus

    pair = 2 * DN_HEAD_DIM
    row = lax.broadcasted_iota(jnp.int32, (CHUNK, LANES), 0)
    lane = lax.broadcasted_iota(jnp.int32, (CHUNK, LANES), 1)
    second = lane >= CHUNK
    col = jnp.where(second, lane - CHUNK, lane)
    incl = row >= col
    strict = row > col
    eye = (row == col).astype(F32)
    trow = lax.broadcasted_iota(jnp.int32, (CHUNK, CHUNK), 0)
    tcol = lax.broadcasted_iota(jnp.int32, (CHUNK, CHUNK), 1)
    tril = (trow >= tcol).astype(F32)
    sel = (lax.broadcasted_iota(jnp.int32, (SUBLANES, LANES), 0)
           == lax.broadcasted_iota(jnp.int32, (SUBLANES, LANES), 1)).astype(F32)

    for ci in range(n_chunks):
        gc = _dot(tril, g[ci * CHUNK:(ci + 1) * CHUNK, :], precision=HIGHEST)
        gc_s[ci * CHUNK:(ci + 1) * CHUNK, :] = gc
        nxt = pltpu.roll(gc, LANES - 1, 1)
        gct_s[ci] = lax.dot_general(sel, jnp.concatenate([gc, nxt], axis=0), NT_DIMS,
                                    precision=HIGHEST, preferred_element_type=F32)
        gl_s[ci:ci + 1, :] = jnp.exp(gc[CHUNK - 1:CHUNK, :])

    def block_diag(m):
        half = m.shape[1] // 2
        z = jnp.zeros_like(m[:, :half])
        return jnp.concatenate([jnp.concatenate([m[:, :half], z], axis=1),
                                jnp.concatenate([z, m[:, half:]], axis=1)], axis=0)

    def block_diag_lanes(m):
        return jnp.concatenate([jnp.where(second, 0.0, m), jnp.where(second, m, 0.0)], axis=0)

    def intra_chunk(it, carry):
        streams = [(it * CHUNKS_PER_ITER + sub, p) for sub in range(CHUNKS_PER_ITER)
                   for p in range(DN_HEADS // 2)]
        st = []
        for ci, p in streams:
            rows = pl.ds(pl.multiple_of(ci * CHUNK, CHUNK), CHUNK)
            ps = slice(p * pair, (p + 1) * pair)
            gc = gc_s[rows, :]
            l0 = DECAY_LANE0 + 2 * p
            gcol0, gcol1 = gc[:, l0:l0 + 1], gc[:, l0 + 1:l0 + 2]
            gcol = jnp.where(second, gcol1, gcol0)
            grow = gct_s[ci][l0:l0 + 1, :]
            decay = jnp.where(incl, jnp.exp(jnp.where(incl, gcol - grow, 0.0)), 0.0)
            q, k, v = q_s[rows, ps], k_s[rows, ps], v_s[rows, ps]
            b0 = beta_s[rows, BETA_LANE0 + 2 * p:BETA_LANE0 + 2 * p + 1]
            b1 = beta_s[rows, BETA_LANE0 + 2 * p + 1:BETA_LANE0 + 2 * p + 2]
            kkqk = lax.dot_general(jnp.concatenate([k, q], axis=0).astype(BF16),
                                   block_diag(k.astype(BF16)), NT_DIMS, preferred_element_type=F32)
            st.append(dict(ci=ci, p=p, rows=rows, ps=ps, gc=gc, l0=l0, gcol0=gcol0, gcol1=gcol1,
                           decay=decay, q=q, k=k, v=v, b0=b0, b1=b1, kkqk=kkqk))
        for s in st:
            a = jnp.where(strict, jnp.where(second, s['b1'], s['b0']) * s['kkqk'][:CHUNK] * s['decay'], 0.0)
            s['inv'] = eye - a
            s['pw'] = _dot(a.astype(BF16), block_diag_lanes(a).astype(BF16))
        for _ in range(4):
            for s in st:
                r = _dot(jnp.concatenate([s['inv'], s['pw']], axis=0).astype(BF16),
                         block_diag_lanes(s['pw']).astype(BF16))
                s['inv'] = s['inv'] + r[:CHUNK]
                s['pw'] = r[CHUNK:]
        for s in st:
            s['inv'] = s['inv'] + _dot(s['inv'].astype(BF16), block_diag_lanes(s['pw']).astype(BF16))
        for s in st:
            eg0, eg1 = jnp.exp(s['gcol0']), jnp.exp(s['gcol1'])
            k, v, q = s['k'], s['v'], s['q']
            hd = DN_HEAD_DIM
            rhs = jnp.concatenate([s['b0'] * v[:, :hd], (s['b0'] * eg0) * k[:, :hd],
                                   s['b1'] * v[:, hd:], (s['b1'] * eg1) * k[:, hd:]], axis=1)
            sol = _dot(s['inv'].astype(BF16), block_diag(rhs.astype(BF16)))
            ci, p = s['ci'], s['p']
            u_s[s['rows'], s['ps']] = jnp.concatenate([sol[:, :hd], sol[:, 2 * hd:3 * hd]], axis=1)
            ws_s[ci, p, 0:CHUNK, :] = jnp.concatenate(
                [sol[:, hd:2 * hd], sol[:, 3 * hd:]], axis=1).astype(BF16)
            ws_s[ci, p, CHUNK:2 * CHUNK, :] = jnp.concatenate(
                [q[:, :hd] * eg0, q[:, hd:] * eg1], axis=1).astype(BF16)
            vn_s[ci, p, 0:CHUNK, :] = (s['kkqk'][CHUNK:] * s['decay']).astype(BF16)
            gc, l0 = s['gc'], s['l0']
            glast0 = gc[CHUNK - 1:CHUNK, l0:l0 + 1]
            glast1 = gc[CHUNK - 1:CHUNK, l0 + 1:l0 + 2]
            kdec = jnp.concatenate([k[:, :hd] * jnp.exp(glast0 - s['gcol0']),
                                    k[:, hd:] * jnp.exp(glast1 - s['gcol1'])], axis=0)
            vn_s[ci, p, CHUNK:CHUNK + hd, :] = jnp.transpose(kdec).astype(BF16)
        return carry

    lax.fori_loop(0, n_chunks // CHUNKS_PER_ITER, intra_chunk, 0)

    def inter_chunk(ci, carry):
        rows = pl.ds(pl.multiple_of(ci * CHUNK, CHUNK), CHUNK)
        glrow = gl_s[pl.ds(ci, 1), :]
        hd = DN_HEAD_DIM
        pairs = range(DN_HEADS // 2)
        r1 = [_dot(ws_s[ci, p], block_diag(jnp.concatenate(
            [state_ref[2 * p], state_ref[2 * p + 1]], axis=1).astype(BF16))) for p in pairs]
        v_new = [u_s[rows, p * pair:(p + 1) * pair] - r1[p][:CHUNK] for p in pairs]
        r2 = [_dot(vn_s[ci, p], block_diag(v_new[p].astype(BF16))) for p in pairs]
        for p in pairs:
            o2 = r1[p][CHUNK:] + r2[p][:CHUNK]
            for j in range(2):
                h = 2 * p + j
                hs = slice(h * hd, (h + 1) * hd)
                lane_h = DECAY_LANE0 + h
                state_ref[h] = (state_ref[h] * glrow[:, lane_h:lane_h + 1]
                                + r2[p][CHUNK:, j * hd:(j + 1) * hd])
                o = o2[:, j * hd:(j + 1) * hd]
                o = o * lax.rsqrt(jnp.mean(o * o, axis=-1, keepdims=True) + EPS) * onorm_ref[...]
                y_ref[rows, hs] = (o * _silu(z_ref[rows, hs])).astype(y_ref.dtype)
        return carry

    lax.fori_loop(0, n_chunks, inter_chunk, 0, unroll=2)


def _deltanet(qkv, z, ba, cw, alog, dtb, onorm, batch, tb=512):
    T = qkv.shape[0]
    nb = T // batch // tb
    n_chunks = tb // CHUNK
    full = lambda a: pl.BlockSpec(a.shape, lambda b, i: (0, 0))
    rows = lambda n: pl.BlockSpec((tb, n), lambda b, i: (b * nb + i, 0))
    scr = lambda *s: pltpu.VMEM(s, F32)
    return pl.pallas_call(
        _deltanet_kernel,
        grid=(batch, nb),
        in_specs=[rows(3 * DN_WIDTH), rows(DN_WIDTH), rows(LANES),
                  full(cw), full(alog), full(dtb), full(onorm)],
        out_specs=rows(DN_WIDTH),
        out_shape=jax.ShapeDtypeStruct((T, DN_WIDTH), BF16),
        scratch_shapes=[
            scr(DN_HEADS, DN_HEAD_DIM, DN_HEAD_DIM),
            scr(SUBLANES, 3 * DN_WIDTH),
            scr(SUBLANES + tb, 3 * DN_WIDTH),
            scr(tb, DN_WIDTH), scr(tb, DN_WIDTH), scr(tb, DN_WIDTH),
            scr(tb, LANES),
            scr(tb, LANES), scr(n_chunks, SUBLANES, 2 * CHUNK), scr(n_chunks, LANES),
            scr(tb, DN_WIDTH),
            pltpu.VMEM((n_chunks, DN_HEADS // 2, 2 * CHUNK, 2 * DN_HEAD_DIM), BF16),
            pltpu.VMEM((n_chunks, DN_HEADS // 2, CHUNK + DN_HEAD_DIM, 2 * CHUNK), BF16),
        ],
        compiler_params=_params("parallel", "arbitrary"),
        name="deltanet",
    )(qkv, z, ba, cw, alog, dtb, onorm)


def _conformer_kernel(glu_ref, wdw_ref, bdw_ref, lng_ref, lnb_ref, y_ref, cbuf_ref):
    tc = glu_ref.shape[0]
    halo = 4 * SUBLANES

    @pl.when(pl.program_id(1) == 0)
    def _():
        cbuf_ref[0:halo, :] = jnp.zeros((halo, CONF_WIDTH), F32)

    u = glu_ref[...]
    g = u[:, :CONF_WIDTH] * _sigmoid(u[:, CONF_WIDTH:])
    cbuf_ref[halo:halo + tc, :] = g
    first = halo - (CONF_KERNEL - 1)
    acc = bdw_ref[...]
    for phase in range(SUBLANES):
        taps = [j for j in range(CONF_KERNEL) if (first + j) % SUBLANES == phase]
        n = tc if phase == 0 else tc + SUBLANES
        part = None
        for j in taps:
            r0 = first + j - phase
            term = wdw_ref[j:j + 1, :] * cbuf_ref[r0:r0 + n, :]
            part = term if part is None else part + term
        acc = acc + part[phase:phase + tc, :]
    cbuf_ref[0:halo, :] = cbuf_ref[tc:tc + halo, :]
    mu = jnp.mean(acc, axis=-1, keepdims=True)
    d = acc - mu
    var = jnp.mean(d * d, axis=-1, keepdims=True)
    y = d * lax.rsqrt(var + EPS) * lng_ref[...] + lnb_ref[...]
    y_ref[...] = _silu(y).astype(y_ref.dtype)


def _conformer(glu, wdw, bdw, lng, lnb, batch, tc=512):
    T = glu.shape[0]
    nb = T // batch // tc
    full = lambda a: pl.BlockSpec(a.shape, lambda b, i: (0, 0))
    rows = lambda n: pl.BlockSpec((tc, n), lambda b, i: (b * nb + i, 0))
    return pl.pallas_call(
        _conformer_kernel,
        grid=(batch, nb),
        in_specs=[rows(2 * CONF_WIDTH), full(wdw), full(bdw), full(lng), full(lnb)],
        out_specs=rows(CONF_WIDTH),
        out_shape=jax.ShapeDtypeStruct((T, CONF_WIDTH), BF16),
        scratch_shapes=[pltpu.VMEM((4 * SUBLANES + tc, CONF_WIDTH), F32)],
        compiler_params=_params("parallel", "arbitrary"),
        name="conformer",
    )(glu, wdw, bdw, lng, lnb)


ROUTE_GATE0, ROUTE_GATE1, ROUTE_EXPERT0, ROUTE_EXPERT1, ROUTE_RANK0, ROUTE_RANK1 = range(6)


def _lane_argmax(vals, valid, lane):
    neg = jnp.float32(-1e30)
    m = jnp.max(jnp.where(valid, vals, neg), axis=-1, keepdims=True)
    idx = jnp.min(jnp.where(valid & (vals == m), lane, LANES), axis=-1, keepdims=True)
    return m, idx


def _out_router_kernel(x_ref, ydn_ref, ycf_ref, wo1_ref, wo2_ref, nw_ref, wrh_ref, wrl_ref, br_ref,
                       x2_ref, h_ref, route_ref, plan_ref, cnt_ref, base_ref):
    tm = x_ref.shape[0]

    @pl.when(pl.program_id(0) == 0)
    def _():
        base_ref[...] = jnp.zeros_like(base_ref)

    x2 = x_ref[...] + _dot(ydn_ref[...], wo1_ref[...]) + _dot(ycf_ref[...], wo2_ref[...])
    x2_ref[...] = x2
    h = _rmsnorm(x2, nw_ref[...])
    _store_token_tiles(h_ref, 0, h)
    h_hi = h.astype(BF16)
    h_lo = (h - h_hi.astype(F32)).astype(BF16)
    hh = _dot(jnp.concatenate([h_hi, h_lo], axis=0), wrh_ref[...])
    logits = hh[:tm] + hh[tm:] + _dot(h_hi, wrl_ref[...]) + br_ref[...]

    lane = lax.broadcasted_iota(jnp.int32, (tm, LANES), 1)
    gvalid = (lane >= GROUP_LANE0) & (lane < GROUP_LANE0 + N_GROUPS)
    gmax, _ = _lane_argmax(logits, gvalid, lane)
    gexp = jnp.where(gvalid, jnp.exp(jnp.where(gvalid, logits - gmax, 0.0)), 0.0)
    p_grp = gexp / jnp.sum(gexp, axis=-1, keepdims=True)
    p_g, g_lane = _lane_argmax(p_grp, gvalid, lane)
    g_sel = g_lane - GROUP_LANE0
    evalid = (lane < N_EXPERTS) & ((lane // EXPERTS_PER_GROUP) == g_sel)
    emax, _ = _lane_argmax(logits, evalid, lane)
    eexp = jnp.where(evalid, jnp.exp(jnp.where(evalid, logits - emax, 0.0)), 0.0)
    p_e = eexp / jnp.sum(eexp, axis=-1, keepdims=True)
    p1, e1 = _lane_argmax(p_e, evalid, lane)
    p2, e2 = _lane_argmax(p_e, evalid & (lane != e1), lane)
    psum = p1 + p2
    gate1 = p_g * p1 / psum
    gate2 = p_g * p2 / psum

    hot1 = (lane == e1).astype(F32)
    hot2 = (lane == e2).astype(F32)
    hot = hot1 + hot2
    trow = lax.broadcasted_iota(jnp.int32, (tm, tm), 0)
    tcol = lax.broadcasted_iota(jnp.int32, (tm, tm), 1)
    before = (trow > tcol).astype(BF16)
    seen = base_ref[...] + _dot(before, hot.astype(BF16))
    rank1 = jnp.sum(hot1 * seen, axis=-1, keepdims=True)
    rank2 = jnp.sum(hot2 * seen, axis=-1, keepdims=True)
    base_ref[...] = base_ref[...] + jnp.sum(hot, axis=0, keepdims=True)
    cnt_ref[...] = base_ref[...]

    route = jnp.zeros((tm, LANES), F32)
    for pos, val in ((ROUTE_GATE0, gate1), (ROUTE_GATE1, gate2),
                     (ROUTE_EXPERT0, e1.astype(F32)), (ROUTE_EXPERT1, e2.astype(F32)),
                     (ROUTE_RANK0, rank1), (ROUTE_RANK1, rank2)):
        route = jnp.where(lane == pos, val, route)
    route_ref[...] = route
    plan_ref[...] = jnp.transpose(route)[:SUBLANES, :]


def _out_router(x, ydn, ycf, wo1, wo2, nw, wr, br, tm=512):
    T, D = x.shape
    full = lambda a: pl.BlockSpec(a.shape, lambda i: (0, 0), pipeline_mode=pl.Buffered(1))
    rows = lambda n: pl.BlockSpec((tm, n), lambda i: (i, 0))
    wr_hi = wr.astype(BF16)
    wr_lo = (wr - wr_hi.astype(F32)).astype(BF16)
    return pl.pallas_call(
        _out_router_kernel,
        grid=(T // tm,),
        in_specs=[rows(D), rows(DN_WIDTH), rows(CONF_WIDTH), full(wo1), full(wo2), full(nw),
                  full(wr_hi), full(wr_lo), full(br)],
        out_specs=[rows(D), pl.BlockSpec((tm * TOKEN_TILE, LANES), lambda i: (i, 0)), rows(LANES),
                   pl.BlockSpec((SUBLANES, tm), lambda i: (0, i)),
                   pl.BlockSpec((1, LANES), lambda i: (0, 0))],
        out_shape=[jax.ShapeDtypeStruct((T, D), F32),
                   jax.ShapeDtypeStruct((T * TOKEN_TILE, LANES), F32),
                   jax.ShapeDtypeStruct((T, LANES), F32), jax.ShapeDtypeStruct((SUBLANES, T), F32),
                   jax.ShapeDtypeStruct((1, LANES), F32)],
        scratch_shapes=[pltpu.VMEM((1, LANES), F32)],
        compiler_params=_params("arbitrary"),
        name="out_router",
    )(x, ydn, ycf, wo1, wo2, nw, wr_hi, wr_lo, br)


def _dispatch_kernel(dest_ref, zb_ref, h_ref, xs_ref, zero_ref, row_sem, zero_sem):
    tm = h_ref.shape[0] // TOKEN_TILE
    base = pl.program_id(0) * tm

    def zero_copy(e):
        start = pl.multiple_of(zb_ref[e] * TOKEN_TILE, SLOT_BLOCK * TOKEN_TILE)
        return pltpu.make_async_copy(zero_ref, xs_ref.at[pl.ds(start, SLOT_BLOCK * TOKEN_TILE)],
                                     zero_sem)

    @pl.when(pl.program_id(0) == 0)
    def _():
        zero_ref[...] = jnp.zeros_like(zero_ref)

        def start(e, carry):
            @pl.when(zb_ref[e] >= 0)
            def _():
                zero_copy(e).start()
            return carry

        def wait(e, carry):
            @pl.when(zb_ref[e] >= 0)
            def _():
                zero_copy(e).wait()
            return carry

        lax.fori_loop(0, 2 * N_EXPERTS, start, 0)
        lax.fori_loop(0, 2 * N_EXPERTS, wait, 0)

    def row_copy(r, k):
        return pltpu.make_async_copy(h_ref.at[_tile_rows(r)],
                                     xs_ref.at[_tile_rows(dest_ref[2 * (base + r) + k])], row_sem)

    def start(r, carry):
        row_copy(r, 0).start(priority=0)
        row_copy(r, 1).start(priority=1)
        return carry

    lax.fori_loop(0, tm, start, 0, unroll=4)
    for _ in range(2):
        pltpu.make_async_copy(h_ref, xs_ref.at[pl.ds(0, tm * TOKEN_TILE)], row_sem).wait()


def _dispatch(dest, zero_blocks, h, n_slots, tm=512):
    T = h.shape[0] // TOKEN_TILE
    return pl.pallas_call(
        _dispatch_kernel,
        grid_spec=pltpu.PrefetchScalarGridSpec(
            num_scalar_prefetch=2,
            grid=(T // tm,),
            in_specs=[pl.BlockSpec((tm * TOKEN_TILE, LANES), lambda i, *_: (i, 0))],
            out_specs=pl.BlockSpec(memory_space=pl.ANY),
            scratch_shapes=[pltpu.VMEM((SLOT_BLOCK * TOKEN_TILE, LANES), F32),
                            pltpu.SemaphoreType.DMA(()), pltpu.SemaphoreType.DMA(())],
        ),
        out_shape=jax.ShapeDtypeStruct((n_slots * TOKEN_TILE, LANES), F32),
        compiler_params=_params("arbitrary"),
        name="dispatch",
    )(dest, zero_blocks, h)


def _experts_kernel(be_ref, nused_ref, xs_ref, wg_ref, wu_ref, wd_ref, ys_ref,
                    wg_s, wu_s, wd_s):
    i = pl.program_id(0)

    @pl.when(i < nused_ref[0])
    def _():
        changed = jnp.logical_or(i == 0, be_ref[i] != be_ref[jnp.maximum(i - 1, 0)])

        @pl.when(changed)
        def _():
            wg_s[...] = wg_ref[0, 0].astype(BF16)
            wu_s[...] = wu_ref[0, 0].astype(BF16)
            wd_s[...] = wd_ref[0, 0].astype(BF16)

        xb = _load_token_tiles(xs_ref, SLOT_BLOCK).astype(BF16)
        act = _silu(_dot(xb, wg_s[...])) * _dot(xb, wu_s[...])
        _store_token_tiles(ys_ref, 0, _dot(act.astype(BF16), wd_s[...]))

    @pl.when(i >= nused_ref[0])
    def _():
        ys_ref[...] = jnp.zeros_like(ys_ref)


def _experts(block_expert, n_used, xs, w_gate, w_up, w_down, layer):
    D, d_expert = w_gate.shape[2:]
    nb = xs.shape[0] // (SLOT_BLOCK * TOKEN_TILE)
    slot_in = lambda i, be, nu: (jnp.minimum(i, nu[0] - 1), 0)
    wsel = lambda i, be, nu: (layer, be[i], 0, 0)
    return pl.pallas_call(
        _experts_kernel,
        grid_spec=pltpu.PrefetchScalarGridSpec(
            num_scalar_prefetch=2,
            grid=(nb,),
            in_specs=[pl.BlockSpec((SLOT_BLOCK * TOKEN_TILE, LANES), slot_in),
                      pl.BlockSpec((1, 1, D, d_expert), wsel),
                      pl.BlockSpec((1, 1, D, d_expert), wsel),
                      pl.BlockSpec((1, 1, d_expert, D), wsel)],
            out_specs=pl.BlockSpec((SLOT_BLOCK * TOKEN_TILE, LANES), lambda i, be, nu: (i, 0)),
            scratch_shapes=[pltpu.VMEM((D, d_expert), BF16), pltpu.VMEM((D, d_expert), BF16),
                            pltpu.VMEM((d_expert, D), BF16)],
        ),
        out_shape=jax.ShapeDtypeStruct(xs.shape, F32),
        compiler_params=_params("arbitrary"),
        name="experts",
    )(block_expert, n_used, xs, w_gate, w_up, w_down)


def _combine_kernel(dest_ref, x_ref, route_ref, nw_ref, ys_ref, o_ref, gbuf_ref, sem, *, final_norm):
    tm = x_ref.shape[0]
    base = pl.program_id(0) * tm

    def row_copy(r, k):
        return pltpu.make_async_copy(ys_ref.at[_tile_rows(dest_ref[2 * (base + r) + k])],
                                     gbuf_ref.at[k, _tile_rows(r)], sem)

    def start(r, carry):
        row_copy(r, 0).start(priority=0)
        row_copy(r, 1).start(priority=1)
        return carry

    lax.fori_loop(0, tm, start, 0, unroll=4)
    for k in range(2):
        pltpu.make_async_copy(ys_ref.at[pl.ds(0, tm * TOKEN_TILE)], gbuf_ref.at[k], sem).wait()
    route = route_ref[...]
    g0 = route[:, ROUTE_GATE0:ROUTE_GATE0 + 1]
    g1 = route[:, ROUTE_GATE1:ROUTE_GATE1 + 1]
    x = x_ref[...] + (_load_token_tiles(gbuf_ref.at[0], tm) * g0 + _load_token_tiles(gbuf_ref.at[1], tm) * g1)
    o_ref[...] = _rmsnorm(x, nw_ref[...]) if final_norm else x


def _combine(dest, x, route, nw, ys, final_norm, tm=512):
    T, D = x.shape
    return pl.pallas_call(
        functools.partial(_combine_kernel, final_norm=final_norm),
        grid_spec=pltpu.PrefetchScalarGridSpec(
            num_scalar_prefetch=1,
            grid=(T // tm,),
            in_specs=[pl.BlockSpec((tm, D), lambda i, d: (i, 0)),
                      pl.BlockSpec((tm, LANES), lambda i, d: (i, 0)),
                      pl.BlockSpec((1, D), lambda i, d: (0, 0)),
                      pl.BlockSpec(memory_space=pl.ANY)],
            out_specs=pl.BlockSpec((tm, D), lambda i, d: (i, 0)),
            scratch_shapes=[pltpu.VMEM((2, tm * TOKEN_TILE, LANES), F32), pltpu.SemaphoreType.DMA(())],
        ),
        out_shape=jax.ShapeDtypeStruct((T, D), F32),
        compiler_params=_params("arbitrary"),
        name="combine",
    )(dest, x, route, nw, ys)


def _slot_plan(plan, counts_row, n_tokens):
    n_blocks = 2 * n_tokens // SLOT_BLOCK + N_EXPERTS
    counts = counts_row[0, :N_EXPERTS].astype(jnp.int32)
    padded = (counts + SLOT_BLOCK - 1) // SLOT_BLOCK * SLOT_BLOCK
    pad_ends = jnp.cumsum(padded)
    pad_starts = pad_ends - padded
    expert = plan[ROUTE_EXPERT0:ROUTE_EXPERT1 + 1].astype(jnp.int32)
    rank = plan[ROUTE_RANK0:ROUTE_RANK1 + 1].astype(jnp.int32)
    hot = expert[..., None] == jnp.arange(N_EXPERTS, dtype=jnp.int32)
    dest = jnp.sum(jnp.where(hot, pad_starts, 0), axis=-1) + rank
    n_used = pad_ends[-1] // SLOT_BLOCK
    first_row = jnp.minimum(jnp.arange(n_blocks, dtype=jnp.int32), n_used - 1) * SLOT_BLOCK
    block_expert = jnp.sum(pad_ends[None, :] <= first_row[:, None], axis=1).astype(jnp.int32)
    block_expert = jnp.minimum(block_expert, N_EXPERTS - 1)
    last_blocks = jnp.where(counts > 0, pad_ends - SLOT_BLOCK, -1)
    tail = n_used + jnp.arange(N_EXPERTS, dtype=jnp.int32)
    tail_blocks = jnp.where(tail < n_blocks, tail * SLOT_BLOCK, -1)
    zero_blocks = jnp.concatenate([last_blocks, tail_blocks]).astype(jnp.int32)
    return (dest.T.reshape(-1), zero_blocks, block_expert, n_used.astype(jnp.int32).reshape(1),
            n_blocks * SLOT_BLOCK)


def _lane_pad(a, lane0, width=LANES):
    return jnp.pad(a.astype(F32), ((0, 0), (lane0, width - lane0 - a.shape[1])))


def kernel(x, norm_mix, w_in, conv_qkv, a_log, dt_bias, o_norm, w_dw, b_dw, ln_g, ln_b,
           w_out, norm_ffn, w_rg, b_rg, w_re, b_re, w_gate, w_up, w_down, norm_final):
    batch, seq, d_model = x.shape
    T = batch * seq
    depth = w_in.shape[0]
    o_z = 3 * DN_WIDTH
    o_b = 4 * DN_WIDTH
    o_c = o_b + 2 * DN_HEADS
    xt = x.reshape(T, d_model)
    for l in range(depth):
        wi = w_in[l]
        qkv, z, ba, glu = _in_proj(
            xt, norm_mix[l][None, :], wi[:, :o_z].astype(BF16), wi[:, o_z:o_b].astype(BF16),
            _lane_pad(wi[:, o_b:o_c], 0).astype(BF16), wi[:, o_c:].astype(BF16))
        y_dn = _deltanet(qkv, z, ba, conv_qkv[l], _lane_pad(a_log[l][None, :], DECAY_LANE0),
                         _lane_pad(dt_bias[l][None, :], DECAY_LANE0), o_norm[l][None, :], batch)
        y_cf = _conformer(glu, w_dw[l], b_dw[l][None, :], ln_g[l][None, :], ln_b[l][None, :], batch)
        wr = jnp.concatenate([w_re[l], _lane_pad(w_rg[l], 0, LANES - N_EXPERTS)], axis=1)
        br = jnp.concatenate([b_re[l][None, :], _lane_pad(b_rg[l][None, :], 0, LANES - N_EXPERTS)], axis=1)
        x2, h, route, plan, counts = _out_router(
            xt, y_dn, y_cf, w_out[l][:DN_WIDTH].astype(BF16), w_out[l][DN_WIDTH:].astype(BF16),
            norm_ffn[l][None, :], wr, br)
        dest, zero_blocks, block_expert, n_used, n_slots = _slot_plan(plan, counts, T)
        xs = _dispatch(dest, zero_blocks, h, n_slots)
        ys = _experts(block_expert, n_used, xs, w_gate, w_up, w_down, l)
        last = l == depth - 1
        xt = _combine(dest, x2, route, norm_final[None, :] if last else norm_ffn[l][None, :],
                      ys, final_norm=last)
    return xt.reshape(batch, seq, d_model)
```
